```python
import math
import jax, jax.numpy as jnp
from jax import lax
import numpy as np

D_MODEL = 2048
BATCH = 4
SEQ = 2048
DEPTH = 2

HEAD_DIM = 128
A_HEADS = 6
A_QK = 64
A_V = 128
B_HEADS = 5
MOBA_BLOCK = 256
MOBA_TOPK = 3
MOBA_QCHUNK = 32
C_HEADS = 5
IDX_HEADS = 16
IDX_DIM = 64
DSA_TOPK = 256
DSA_QBLOCK = 128
DENSE_QBLOCK = 128
ROPE_THETA = 500000.0
ROPE_FRAC = 4
EPS = 1e-6
N_GROUPS = 4
EXPERTS_PER_GROUP = 8
N_EXPERTS = N_GROUPS * EXPERTS_PER_GROUP
EXPERT_FF = 512
TOPK_IN_GROUP = 2

IN_SIZES = (
    A_HEADS * 2 * A_QK,
    A_HEADS * 2 * A_QK,
    A_HEADS * A_V,
    B_HEADS * HEAD_DIM,
    B_HEADS * HEAD_DIM,
    B_HEADS * HEAD_DIM,
    C_HEADS * HEAD_DIM,
    HEAD_DIM,
    HEAD_DIM,
    IDX_HEADS * IDX_DIM,
    IDX_DIM,
    IDX_HEADS,
)
IN_DIM = sum(IN_SIZES)
IN_OFFSETS = [int(v) for v in np.cumsum(IN_SIZES)[:-1]]
MIX_WIDTH = A_HEADS * A_V + B_HEADS * HEAD_DIM + C_HEADS * HEAD_DIM

kernel_name = "hybrid_diff_moba_dsa_hmoe"


def rms_norm(x, g):
    xf = x.astype(jnp.float32)
    y = xf * lax.rsqrt(jnp.mean(xf * xf, axis=-1, keepdims=True) + EPS)
    return (y * g.astype(jnp.float32)).astype(x.dtype)


def rope_tables(seq, dim):
    rot = dim // ROPE_FRAC
    inv = 1.0 / (ROPE_THETA ** (jnp.arange(0, rot, 2, dtype=jnp.float32) / rot))
    ang = jnp.arange(seq, dtype=jnp.float32)[:, None] * inv[None, :]
    return jnp.cos(ang), jnp.sin(ang)


def apply_partial_rope(x, cos, sin):
    half = cos.shape[-1]
    rot = 2 * half
    x1, x2, xp = x[..., :half], x[..., half:rot], x[..., rot:]
    c = cos.astype(x.dtype)
    s = sin.astype(x.dtype)
    return jnp.concatenate([x1 * c - x2 * s, x2 * c + x1 * s, xp], axis=-1)


def diff_attention(q, k, v, lam):
    b_, h_, _, s_, dq = q.shape
    nqb = s_ // DENSE_QBLOCK
    scale = dq ** -0.5
    q_blocks = jnp.moveaxis(q.reshape(b_, h_, 2, nqb, DENSE_QBLOCK, dq), 3, 0)
    kpos = jnp.arange(s_)

    def block(args):
        qblk, i = args
        qpos = i * DENSE_QBLOCK + jnp.arange(DENSE_QBLOCK)
        s = jnp.einsum('bhcqd,bhckd->bhcqk', qblk, k).astype(jnp.float32) * scale
        s = jnp.where(kpos[None, :] <= qpos[:, None], s, -jnp.inf)
        p = jax.nn.softmax(s, axis=-1)
        a = p[:, :, 0] - lam * p[:, :, 1]
        return jnp.einsum('bhqk,bhkd->bhqd', a.astype(v.dtype), v)

    o = lax.map(block, (q_blocks, jnp.arange(nqb)))
    return jnp.moveaxis(o, 0, 2).reshape(b_, h_, s_, v.shape[-1])


def moba_attention(q, k, v):
    b_, h_, s_, d = q.shape
    nblk = -(-s_ // MOBA_BLOCK)
    s_pad = nblk * MOBA_BLOCK
    pad = ((0, 0), (0, 0), (0, s_pad - s_), (0, 0))
    k_pad = jnp.pad(k, pad)
    v_pad = jnp.pad(v, pad)
    kb = k_pad.reshape(b_, h_, nblk, MOBA_BLOCK, d)
    vb = v_pad.reshape(b_, h_, nblk, MOBA_BLOCK, d)
    k_mean = jnp.mean(kb.astype(jnp.float32), axis=3)
    qpos = jnp.arange(s_)
    own_blk = qpos // MOBA_BLOCK
    gate = jnp.einsum('bhsd,bhnd->bhsn', q.astype(jnp.float32), k_mean)
    fully_past = jnp.arange(nblk)[None, :] < own_blk[:, None]
    gate = jnp.where(fully_past, gate, -jnp.inf)
    n_sel = min(MOBA_TOPK, nblk)
    _, sel = lax.top_k(gate, n_sel)
    sel_valid = jnp.arange(n_sel)[None, :] < jnp.minimum(own_blk, MOBA_TOPK)[:, None]
    scale = d ** -0.5
    nqc = s_ // MOBA_QCHUNK
    q_c = jnp.moveaxis(q.reshape(b_, h_, nqc, MOBA_QCHUNK, d), 2, 0)
    sel_c = jnp.moveaxis(sel.reshape(b_, h_, nqc, MOBA_QCHUNK, n_sel), 2, 0)
    valid_c = sel_valid.reshape(nqc, MOBA_QCHUNK, n_sel)
    bi = jnp.arange(b_)[:, None, None, None]
    hi = jnp.arange(h_)[None, :, None, None]
    own_off = jnp.arange(MOBA_BLOCK)

    def chunk(args):
        qc, sc, vc, i = args
        start = i * MOBA_QCHUNK
        qp = start + jnp.arange(MOBA_QCHUNK)
        kg = kb[bi, hi, sc]
        vg = vb[bi, hi, sc]
        s_past = jnp.einsum('bhqd,bhqnkd->bhqnk', qc, kg).astype(jnp.float32) * scale
        s_past = jnp.where(vc[None, None, :, :, None], s_past, -jnp.inf)
        s_past = s_past.reshape(b_, h_, MOBA_QCHUNK, n_sel * MOBA_BLOCK)
        own_start = (start // MOBA_BLOCK) * MOBA_BLOCK
        k_own = lax.dynamic_slice_in_dim(k_pad, own_start, MOBA_BLOCK, axis=2)
        v_own = lax.dynamic_slice_in_dim(v_pad, own_start, MOBA_BLOCK, axis=2)
        s_own = jnp.einsum('bhqd,bhkd->bhqk', qc, k_own).astype(jnp.float32) * scale
        s_own = jnp.where((own_start + own_off)[None, :] <= qp[:, None], s_own, -jnp.inf)
        p = jax.nn.softmax(jnp.concatenate([s_own, s_past], axis=-1), axis=-1).astype(v.dtype)
        p_own = p[..., :MOBA_BLOCK]
        p_past = p[..., MOBA_BLOCK:].reshape(b_, h_, MOBA_QCHUNK, n_sel, MOBA_BLOCK)
        return (jnp.einsum('bhqk,bhkd->bhqd', p_own, v_own)
                + jnp.einsum('bhqnk,bhqnkd->bhqd', p_past, vg))

    o = lax.map(chunk, (q_c, sel_c, valid_c, jnp.arange(nqc)))
    return jnp.moveaxis(o, 0, 2).reshape(b_, h_, s_, d)


def dsa_attention(q, k, v, qi, ki, wi):
    b_, h_, s_, d = q.shape
    n_sel = min(DSA_TOPK, s_ // 4)
    nqb = s_ // DSA_QBLOCK
    kpos = jnp.arange(s_)
    scale = d ** -0.5
    q_b = jnp.moveaxis(q.reshape(b_, h_, nqb, DSA_QBLOCK, d), 2, 0)
    qi_b = jnp.moveaxis(qi.reshape(b_, IDX_HEADS, nqb, DSA_QBLOCK, IDX_DIM), 2, 0)
    wi_b = jnp.moveaxis(wi.reshape(b_, nqb, DSA_QBLOCK, IDX_HEADS), 1, 0)

    def take_rows(arr, idx):
        return jax.vmap(lambda a, ix: a[ix])(arr, idx)

    def block(args):
        qb, qib, wb, i = args
        qp = i * DSA_QBLOCK + jnp.arange(DSA_QBLOCK)
        logits = jnp.einsum('bhqd,bsd->bhqs', qib, ki).astype(jnp.float32)
        idx_score = jnp.einsum('bqh,bhqs->bqs', wb.astype(jnp.float32), jax.nn.relu(logits))
        idx_score = jnp.where(kpos[None, :] <= qp[:, None], idx_score, -jnp.inf)
        top_val, top_idx = lax.top_k(idx_score, n_sel)
        valid = jnp.isfinite(top_val)
        kg = take_rows(k, top_idx)
        vg = take_rows(v, top_idx)
        s = jnp.einsum('bhqd,bqnd->bhqn', qb, kg).astype(jnp.float32) * scale
        s = jnp.where(valid[:, None], s, -jnp.inf)
        p = jax.nn.softmax(s, axis=-1).astype(v.dtype)
        return jnp.einsum('bhqn,bqnd->bhqd', p, vg)

    o = lax.map(block, (q_b, qi_b, wi_b, jnp.arange(nqb)))
    return jnp.moveaxis(o, 0, 2).reshape(b_, h_, s_, d)


def hier_moe(x, w_group, b_group, w_expert, b_expert, w_gate, w_up, w_down):
    b_, s_, dm = x.shape
    t = x.reshape(b_ * s_, dm)
    n_tok = t.shape[0]
    g_logits = (t @ w_group).astype(jnp.float32) + b_group.astype(jnp.float32)
    g_prob = jax.nn.softmax(g_logits, axis=-1)
    g_idx = jnp.argmax(g_logits, axis=-1)
    g_w = jnp.take_along_axis(g_prob, g_idx[:, None], axis=1)[:, 0]
    e_logits = (t @ w_expert).astype(jnp.float32) + b_expert.astype(jnp.float32)
    e_logits = e_logits.reshape(n_tok, N_GROUPS, EXPERTS_PER_GROUP)
    e_in_group = jnp.take_along_axis(e_logits, g_idx[:, None, None], axis=1)[:, 0]
    e_prob = jax.nn.softmax(e_in_group, axis=-1)
    top_v, top_i = lax.top_k(e_prob, TOPK_IN_GROUP)
    top_v = top_v / jnp.sum(top_v, axis=-1, keepdims=True)
    gate_vals = g_w[:, None] * top_v
    expert_id = g_idx[:, None] * EXPERTS_PER_GROUP + top_i
    combine = jnp.sum(jax.nn.one_hot(expert_id, N_EXPERTS, dtype=jnp.float32)
                      * gate_vals[..., None], axis=1).astype(x.dtype)
    hdn = (jax.nn.silu(jnp.einsum('td,edf->tef', t, w_gate))
           * jnp.einsum('td,edf->tef', t, w_up))
    y = jnp.einsum('tef,efd->td', hdn * combine[:, :, None], w_down)
    return y.reshape(b_, s_, dm)


def setup_inputs(seed: int = 0) -> dict:
    key = jax.random.key(seed)
    ks = jax.random.split(key, 21)
    f32 = jnp.float32

    def nrm(k, shape, scale):
        return jax.random.normal(k, shape, f32) * scale

    def gain(k, shape):
        return 1.0 + 0.05 * jax.random.normal(k, shape, f32)

    return {
        "x": nrm(ks[0], (BATCH, SEQ, D_MODEL), 1.0),
        "norm1_g": gain(ks[1], (DEPTH, D_MODEL)),
        "w_in": nrm(ks[2], (DEPTH, D_MODEL, IN_DIM), D_MODEL ** -0.5),
        "a_qn_g": gain(ks[3], (DEPTH, A_QK)),
        "a_kn_g": gain(ks[4], (DEPTH, A_QK)),
        "a_lambda": nrm(ks[5], (DEPTH, 4, A_QK), 0.1),
        "a_subln_g": gain(ks[6], (DEPTH, A_V)),
        "b_qn_g": gain(ks[7], (DEPTH, HEAD_DIM)),
        "b_kn_g": gain(ks[8], (DEPTH, HEAD_DIM)),
        "c_qn_g": gain(ks[9], (DEPTH, HEAD_DIM)),
        "c_kn_g": gain(ks[10], (DEPTH, HEAD_DIM)),
        "idx_kn_g": gain(ks[11], (DEPTH, IDX_DIM)),
        "w_out": nrm(ks[12], (DEPTH, MIX_WIDTH, D_MODEL), MIX_WIDTH ** -0.5),
        "norm2_g": gain(ks[13], (DEPTH, D_MODEL)),
        "w_group": nrm(ks[14], (DEPTH, D_MODEL, N_GROUPS), D_MODEL ** -0.5),
        "b_group": nrm(ks[15], (DEPTH, N_GROUPS), 0.01),
        "w_expert": nrm(ks[16], (DEPTH, D_MODEL, N_EXPERTS), D_MODEL ** -0.5),
        "b_expert": nrm(ks[17], (DEPTH, N_EXPERTS), 0.01),
        "w_gate": nrm(ks[18], (DEPTH, N_EXPERTS, D_MODEL, EXPERT_FF), D_MODEL ** -0.5),
        "w_up": nrm(ks[19], (DEPTH, N_EXPERTS, D_MODEL, EXPERT_FF), D_MODEL ** -0.5),
        "w_down": nrm(ks[20], (DEPTH, N_EXPERTS, EXPERT_FF, D_MODEL), EXPERT_FF ** -0.5),
    }


def reference(x, norm1_g, w_in, a_qn_g, a_kn_g, a_lambda, a_subln_g, b_qn_g, b_kn_g,
              c_qn_g, c_kn_g, idx_kn_g, w_out, norm2_g, w_group, b_group, w_expert,
              b_expert, w_gate, w_up, w_down):
    b_, s_, _ = x.shape
    cos64, sin64 = rope_tables(s_, A_QK)
    cos128, sin128 = rope_tables(s_, HEAD_DIM)
    idx_w_scale = (IDX_HEADS ** -0.5) * (IDX_DIM ** -0.5)

    for l in range(DEPTH):
        h = rms_norm(x, norm1_g[l])
        proj = h @ w_in[l]
        (qa, ka, va, qb, kb, vb, qc, kc, vc, qi, ki, wi) = jnp.split(proj, IN_OFFSETS, axis=-1)

        qa = qa.reshape(b_, s_, A_HEADS, 2, A_QK).transpose(0, 2, 3, 1, 4)
        ka = ka.reshape(b_, s_, A_HEADS, 2, A_QK).transpose(0, 2, 3, 1, 4)
        qa = apply_partial_rope(rms_norm(qa, a_qn_g[l]), cos64, sin64)
        ka = apply_partial_rope(rms_norm(ka, a_kn_g[l]), cos64, sin64)
        va = va.reshape(b_, s_, A_HEADS, A_V).transpose(0, 2, 1, 3)
        lam_init = 0.8 - 0.6 * math.exp(-0.3 * l)
        lp = a_lambda[l].astype(jnp.float32)
        lam = jnp.exp(jnp.sum(lp[0] * lp[1])) - jnp.exp(jnp.sum(lp[2] * lp[3])) + lam_init
        oa = diff_attention(qa, ka, va, lam)
        oa = rms_norm(oa, a_subln_g[l]) * (1.0 - lam_init)
        oa = oa.transpose(0, 2, 1, 3).reshape(b_, s_, A_HEADS * A_V)

        qb = qb.reshape(b_, s_, B_HEADS, HEAD_DIM).transpose(0, 2, 1, 3)
        kb = kb.reshape(b_, s_, B_HEADS, HEAD_DIM).transpose(0, 2, 1, 3)
        vb = vb.reshape(b_, s_, B_HEADS, HEAD_DIM).transpose(0, 2, 1, 3)
        qb = apply_partial_rope(rms_norm(qb, b_qn_g[l]), cos128, sin128)
        kb = apply_partial_rope(rms_norm(kb, b_kn_g[l]), cos128, sin128)
        ob = moba_attention(qb, kb, vb)
        ob = ob.transpose(0, 2, 1, 3).reshape(b_, s_, B_HEADS * HEAD_DIM)

        qc = qc.reshape(b_, s_, C_HEADS, HEAD_DIM).transpose(0, 2, 1, 3)
        qc = apply_partial_rope(rms_norm(qc, c_qn_g[l]), cos128, sin128)
        kc = apply_partial_rope(rms_norm(kc, c_kn_g[l]), cos128, sin128)
        qi = qi.reshape(b_, s_, IDX_HEADS, IDX_DIM).transpose(0, 2, 1, 3)
        qi = apply_partial_rope(qi, cos64, sin64)
        ki = apply_partial_rope(rms_norm(ki, idx_kn_g[l]), cos64, sin64)
        wi = wi * idx_w_scale
        oc = dsa_attention(qc, kc, vc, qi, ki, wi)
        oc = oc.transpose(0, 2, 1, 3).reshape(b_, s_, C_HEADS * HEAD_DIM)

        mix = jnp.concatenate([oa, ob, oc], axis=-1) @ w_out[l]
        x = x + mix

        x = x + hier_moe(rms_norm(x, norm2_g[l]), w_group[l], b_group[l], w_expert[l],
                         b_expert[l], w_gate[l], w_up[l], w_down[l])
    return x
```

```python
import functools
import math

import jax
import jax.numpy as jnp
from jax import lax
from jax.experimental import pallas as pl
from jax.experimental.pallas import tpu as pltpu

F32 = jnp.float32
BF16 = jnp.bfloat16
I32 = jnp.int32

D_MODEL = 2048
HEAD_DIM = 128
A_HEADS = 6
A_QK = 64
B_HEADS = 5
MOBA_BLOCK = 256
MOBA_TOPK = 3
C_HEADS = 5
IDX_HEADS = 16
IDX_DIM = 64
DSA_TOPK = 256
ROPE_THETA = 500000.0
ROPE_FRAC = 4
EPS = 1e-6
N_GROUPS = 4
EXPERTS_PER_GROUP = 8
N_EXPERTS = N_GROUPS * EXPERTS_PER_GROUP
EXPERT_FF = 512

LANES = 128
MAIN_COLS = 6144
TAIL_COLS = 80
QA_C, KA_C, VA_C = 0, 6, 12
QB_C, KB_C, VB_C = 18, 23, 28
QC_C, KC_C, VC_C = 33, 38, 39
QI_C = 40
NEG = -1e30
INT_MIN = -(2 ** 31)
MOE_TILE = 256
VMEM_LIMIT = 56 * 1024 * 1024

_NT = (((1,), (1,)), ((), ()))


def _cp(sem, vmem=VMEM_LIMIT):
    return pltpu.CompilerParams(dimension_semantics=sem, vmem_limit_bytes=vmem)


def _rms_full(x, g):
    ms = jnp.mean(x * x, axis=-1, keepdims=True)
    return x * lax.rsqrt(ms + EPS) * g


def _rms_halves(x, g):
    lane = lax.broadcasted_iota(I32, x.shape, 1)
    lo = lane < 64
    x2 = x * x
    s_lo = jnp.sum(jnp.where(lo, x2, 0.0), axis=-1, keepdims=True)
    s_hi = jnp.sum(jnp.where(lo, 0.0, x2), axis=-1, keepdims=True)
    r = jnp.where(lo, lax.rsqrt(s_lo * (1.0 / 64) + EPS), lax.rsqrt(s_hi * (1.0 / 64) + EPS))
    return x * r * g


def _rope(x, tab_ref, half):
    c = tab_ref[0]
    sa = tab_ref[1]
    sb = tab_ref[2]
    return x * c + pltpu.roll(x, LANES - half, 1) * sa + pltpu.roll(x, half, 1) * sb


def _flash_update(s, v, carry):
    m, l, acc = carry
    m_new = jnp.maximum(m, jnp.max(s, axis=-1, keepdims=True))
    alpha = jnp.exp(m - m_new)
    p = jnp.exp(s - m_new)
    l_new = alpha * l + jnp.sum(p, axis=-1, keepdims=True)
    acc_new = alpha * acc + jnp.dot(p.astype(BF16), v, preferred_element_type=F32)
    return m_new, l_new, acc_new


def _flash_init(rows, width):
    return (jnp.full((rows, 1), NEG, F32), jnp.zeros((rows, 1), F32), jnp.zeros((rows, width), F32))


def _norm_kernel(x_ref, g_ref, o_ref):
    o_ref[...] = _rms_full(x_ref[...], g_ref[...]).astype(o_ref.dtype)


def _rmsnorm_bf16(x, g):
    t, d = x.shape
    tm = 512
    return pl.pallas_call(
        _norm_kernel,
        grid=(t // tm,),
        in_specs=[pl.BlockSpec((tm, d), lambda i: (i, 0)), pl.BlockSpec((1, d), lambda i: (0, 0))],
        out_specs=pl.BlockSpec((tm, d), lambda i: (i, 0)),
        out_shape=jax.ShapeDtypeStruct((t, d), BF16),
        compiler_params=_cp(("parallel",)),
        name="rmsnorm",
    )(x, g.reshape(1, d))


def _cast_kernel(x_ref, o_ref):
    o_ref[...] = x_ref[...].astype(o_ref.dtype)


def _cast_bf16(w, layer, ncols):
    r = w.shape[1]
    tr, tc = 256, min(ncols, 1024)
    return pl.pallas_call(
        _cast_kernel,
        grid=(r // tr, ncols // tc),
        in_specs=[pl.BlockSpec((None, tr, tc), lambda i, j: (layer, i, j))],
        out_specs=pl.BlockSpec((tr, tc), lambda i, j: (i, j)),
        out_shape=jax.ShapeDtypeStruct((r, ncols), BF16),
        compiler_params=_cp(("parallel", "parallel")),
        name="cast_bf16",
    )(w)


def _matmul_kernel(x_ref, w_ref, o_ref):
    o_ref[...] = jnp.dot(x_ref[...], w_ref[...], preferred_element_type=F32).astype(o_ref.dtype)


def _matmul(x, w, out_dtype, tm, tn, name):
    m, k = x.shape
    n = w.shape[1]
    return pl.pallas_call(
        _matmul_kernel,
        grid=(n // tn, m // tm),
        in_specs=[pl.BlockSpec((tm, k), lambda j, i: (i, 0)), pl.BlockSpec((k, tn), lambda j, i: (0, j))],
        out_specs=pl.BlockSpec((tm, tn), lambda j, i: (i, j)),
        out_shape=jax.ShapeDtypeStruct((m, n), out_dtype),
        compiler_params=_cp(("parallel", "parallel")),
        name=name,
    )(x, w)


def _attn_a_kernel(q_ref, k_ref, v_ref, tabq_ref, tabk_ref, qg_ref, kg_ref, lam_ref, sg_ref,
                   o_ref, ks_ref, *, tq, lam_init):
    qt = pl.program_id(2)

    @pl.when(qt == 0)
    def _():
        k = _rms_halves(k_ref[...].astype(F32), kg_ref[...])
        ks_ref[...] = _rope(k, tabk_ref, 8).astype(BF16)

    q = _rms_halves(q_ref[...].astype(F32), qg_ref[...])
    q = _rope(q, tabq_ref, 8) * (A_QK ** -0.5)
    lane = lax.broadcasted_iota(I32, q.shape, 1)
    q1 = jnp.where(lane < 64, q, 0.0).astype(BF16)
    q2 = jnp.where(lane < 64, 0.0, q).astype(BF16)

    lp = lam_ref[...]
    lam = (jnp.exp(jnp.sum(lp[0:1] * lp[1:2], axis=-1, keepdims=True))
           - jnp.exp(jnp.sum(lp[2:3] * lp[3:4], axis=-1, keepdims=True)) + lam_init)

    def step(c, carry, mask):
        off = pl.multiple_of(c * tq, tq)
        kc = ks_ref[pl.ds(off, tq), :]
        vc = v_ref[pl.ds(off, tq), :]
        out = []
        for qq, cr in zip((q1, q2), carry):
            s = lax.dot_general(qq, kc, _NT, preferred_element_type=F32)
            if mask is not None:
                s = jnp.where(mask, s, NEG)
            out.append(_flash_update(s, vc, cr))
        return tuple(out)

    init = (_flash_init(tq, HEAD_DIM), _flash_init(tq, HEAD_DIM))
    carry = lax.fori_loop(0, qt, lambda c, cr: step(c, cr, None), init)
    row = lax.broadcasted_iota(I32, (tq, tq), 0)
    col = lax.broadcasted_iota(I32, (tq, tq), 1)
    (_, l1, a1), (_, l2, a2) = step(qt, carry, col <= row)

    o = a1 / l1 - lam * (a2 / l2)
    o = _rms_full(o, sg_ref[...]) * (1.0 - lam_init)
    o_ref[...] = o.astype(o_ref.dtype)


def _attn_a(proj, tab64, qg, kg, lam_p, sg, *, b, s, lam_init):
    tq = 256
    nq = s // tq
    kern = functools.partial(_attn_a_kernel, tq=tq, lam_init=lam_init)
    small = lambda shape: pl.BlockSpec(shape, lambda bi, h, qt: (0,) * len(shape))
    return pl.pallas_call(
        kern,
        grid=(b, A_HEADS, nq),
        in_specs=[
            pl.BlockSpec((tq, LANES), lambda bi, h, qt: (bi * nq + qt, QA_C + h)),
            pl.BlockSpec((s, LANES), lambda bi, h, qt: (bi, KA_C + h)),
            pl.BlockSpec((s, LANES), lambda bi, h, qt: (bi, VA_C + h)),
            pl.BlockSpec((3, tq, LANES), lambda bi, h, qt: (0, qt, 0)),
            pl.BlockSpec((3, s, LANES), lambda bi, h, qt: (0, 0, 0)),
            small((1, LANES)), small((1, LANES)), small((4, A_QK)), small((1, LANES)),
        ],
        out_specs=pl.BlockSpec((tq, LANES), lambda bi, h, qt: (bi * nq + qt, h)),
        out_shape=jax.ShapeDtypeStruct((b * s, A_HEADS * HEAD_DIM), BF16),
        scratch_shapes=[pltpu.VMEM((s, LANES), BF16)],
        compiler_params=_cp(("parallel", "parallel", "arbitrary")),
        name="attn_diff",
    )(proj, proj, proj, tab64, tab64, qg, kg, lam_p, sg)


def _attn_b_kernel(q_ref, k_ref, v_ref, tabq_ref, tabk_ref, qg_ref, kg_ref,
                   o_ref, ks_ref, kmean_ref, *, tq, nblk):
    qt = pl.program_id(2)

    @pl.when(qt == 0)
    def _():
        k = _rope(_rms_full(k_ref[...].astype(F32), kg_ref[...]), tabk_ref, 16)
        ks_ref[...] = k.astype(BF16)
        kmean_ref[...] = jnp.zeros_like(kmean_ref)
        kmean_ref[0:nblk, :] = jnp.mean(k.reshape(nblk, MOBA_BLOCK, HEAD_DIM), axis=1)

    q = _rope(_rms_full(q_ref[...].astype(F32), qg_ref[...]), tabq_ref, 16)

    gate = lax.dot_general(q, kmean_ref[...], _NT, preferred_element_type=F32,
                           precision=lax.Precision.HIGHEST)
    lane = lax.broadcasted_iota(I32, gate.shape, 1)
    g = jnp.where(lane < qt, gate, -jnp.inf)
    sel = jnp.zeros(gate.shape, F32)
    for _ in range(MOBA_TOPK):
        mx = jnp.max(g, axis=-1, keepdims=True)
        first = jnp.min(jnp.where(g == mx, lane, LANES), axis=-1, keepdims=True)
        pick = jnp.logical_and(lane == first, mx > -jnp.inf)
        sel = jnp.where(pick, 1.0, sel)
        g = jnp.where(pick, -jnp.inf, g)

    qs = (q * (HEAD_DIM ** -0.5)).astype(BF16)

    def step(c, carry, mask):
        off = pl.multiple_of(c * tq, tq)
        s = lax.dot_general(qs, ks_ref[pl.ds(off, tq), :], _NT, preferred_element_type=F32)
        s = jnp.where(mask, s, NEG)
        return _flash_update(s, v_ref[pl.ds(off, tq), :], carry)

    row = lax.broadcasted_iota(I32, (tq, tq), 0)
    col = lax.broadcasted_iota(I32, (tq, tq), 1)
    carry = step(qt, _flash_init(tq, HEAD_DIM), col <= row)

    def past(c, carry):
        selc = jnp.max(jnp.where(lane == c, sel, 0.0), axis=-1, keepdims=True) > 0.5
        return step(c, carry, selc)

    _, l, acc = lax.fori_loop(0, qt, past, carry)
    o_ref[...] = (acc / l).astype(o_ref.dtype)


def _attn_b(proj, tab128, qg, kg, *, b, s):
    tq = MOBA_BLOCK
    nq = s // tq
    kern = functools.partial(_attn_b_kernel, tq=tq, nblk=nq)
    small = lambda shape: pl.BlockSpec(shape, lambda bi, h, qt: (0,) * len(shape))
    return pl.pallas_call(
        kern,
        grid=(b, B_HEADS, nq),
        in_specs=[
            pl.BlockSpec((tq, LANES), lambda bi, h, qt: (bi * nq + qt, QB_C + h)),
            pl.BlockSpec((s, LANES), lambda bi, h, qt: (bi, KB_C + h)),
            pl.BlockSpec((s, LANES), lambda bi, h, qt: (bi, VB_C + h)),
            pl.BlockSpec((3, tq, LANES), lambda bi, h, qt: (0, qt, 0)),
            pl.BlockSpec((3, s, LANES), lambda bi, h, qt: (0, 0, 0)),
            small((1, LANES)), small((1, LANES)),
        ],
        out_specs=pl.BlockSpec((tq, LANES), lambda bi, h, qt: (bi * nq + qt, h)),
        out_shape=jax.ShapeDtypeStruct((b * s, B_HEADS * HEAD_DIM), BF16),
        scratch_shapes=[pltpu.VMEM((s, LANES), BF16), pltpu.VMEM((LANES, LANES), F32)],
        compiler_params=_cp(("parallel", "parallel", "arbitrary")),
        name="attn_moba",
    )(proj, proj, proj, tab128, tab128, qg, kg)


def _attn_c_kernel(q0_ref, q1_ref, q2_ref, q3_ref, q4_ref, qi_ref, kc_ref, vc_ref, kt_ref, wt_ref,
                   t128q_ref, t128k_ref, t64q_ref, t64k_ref, cqg_ref, ckg_ref, ikg_ref,
                   o_ref, kcs_ref, kis_ref, keys_ref, *, tq, tk, n_sel):
    qt = pl.program_id(1)

    @pl.when(qt == 0)
    def _():
        k = _rope(_rms_full(kc_ref[...].astype(F32), ckg_ref[...]), t128k_ref, 16)
        kcs_ref[...] = k.astype(BF16)
        t = kt_ref[...]
        lane = lax.broadcasted_iota(I32, t.shape, 1)
        ms = jnp.sum(jnp.where(lane < IDX_DIM, t * t, 0.0), axis=-1, keepdims=True) * (1.0 / IDX_DIM)
        ki = _rope(t * lax.rsqrt(ms + EPS) * ikg_ref[...], t64k_ref, 8)
        kis_ref[...] = (ki + pltpu.roll(ki, 64, 1)).astype(BF16)

    lane = lax.broadcasted_iota(I32, (tq, LANES), 1)
    heads = []
    for j in range(IDX_HEADS // 2):
        x = _rope(qi_ref[:, j * LANES:(j + 1) * LANES].astype(F32), t64q_ref, 8)
        heads.append(jnp.where(lane < 64, x, 0.0).astype(BF16))
        heads.append(jnp.where(lane < 64, 0.0, x).astype(BF16))
    qi_all = jnp.concatenate(heads, axis=0)
    wt = wt_ref[...] * ((IDX_HEADS ** -0.5) * (IDX_DIM ** -0.5))
    w_cols = [wt[:, IDX_DIM + h:IDX_DIM + h + 1] for h in range(IDX_HEADS)]

    nch = (qt * tq + tq + tk - 1) // tk
    row = qt * tq + lax.broadcasted_iota(I32, (tq, tk), 0)
    col0 = lax.broadcasted_iota(I32, (tq, tk), 1)

    def score_chunk(c, _):
        off = pl.multiple_of(c * tk, tk)
        lg = lax.dot_general(qi_all, kis_ref[pl.ds(off, tk), :], _NT, preferred_element_type=F32)
        sc = jnp.zeros((tq, tk), F32)
        for h in range(IDX_HEADS):
            sc = sc + jnp.maximum(lg[h * tq:(h + 1) * tq], 0.0) * w_cols[h]
        bits = pltpu.bitcast(sc, I32)
        key = jnp.where(bits < 0, bits ^ jnp.int32(0x7FFFFFFF), bits)
        keys_ref[c] = jnp.where(col0 + c * tk <= row, key, jnp.int32(INT_MIN))
        return 0

    lax.fori_loop(0, nch, score_chunk, 0)

    def count_ge(cand):
        def body(c, acc):
            return acc + jnp.where(keys_ref[c] >= cand, 1.0, 0.0)
        tot = lax.fori_loop(0, nch, body, jnp.zeros((tq, tk), F32))
        return jnp.sum(tot, axis=-1, keepdims=True)

    lo = jnp.where(count_ge(jnp.zeros((tq, 1), I32)) >= n_sel, jnp.int32(0), jnp.int32(INT_MIN))

    def bit_step(i, lo):
        cand = lo + lax.shift_left(jnp.int32(1), 30 - i)
        return jnp.where(count_ge(cand) >= n_sel, cand, lo)

    lo = lax.fori_loop(0, 31, bit_step, lo)
    thr = jnp.maximum(lo, jnp.int32(INT_MIN + 1))

    qs = []
    for q_ref in (q0_ref, q1_ref, q2_ref, q3_ref, q4_ref):
        q = _rope(_rms_full(q_ref[...].astype(F32), cqg_ref[...]), t128q_ref, 16)
        qs.append((q * (HEAD_DIM ** -0.5)).astype(BF16))
    q_all = jnp.concatenate(qs, axis=0)

    def attn_chunk(c, carry):
        m, l, acc = carry
        off = pl.multiple_of(c * tk, tk)
        s = lax.dot_general(q_all, kcs_ref[pl.ds(off, tk), :], _NT, preferred_element_type=F32)
        s = s.reshape(C_HEADS, tq, tk)
        s = jnp.where((keys_ref[c] >= thr)[None], s, NEG)
        m_new = jnp.maximum(m, jnp.max(s, axis=-1, keepdims=True))
        alpha = jnp.exp(m - m_new)
        p = jnp.exp(s - m_new)
        l_new = alpha * l + jnp.sum(p, axis=-1, keepdims=True)
        pv = jnp.dot(p.reshape(C_HEADS * tq, tk).astype(BF16), vc_ref[pl.ds(off, tk), :],
                     preferred_element_type=F32)
        return m_new, l_new, alpha * acc + pv.reshape(C_HEADS, tq, HEAD_DIM)

    init = (jnp.full((C_HEADS, tq, 1), NEG, F32), jnp.zeros((C_HEADS, tq, 1), F32),
            jnp.zeros((C_HEADS, tq, HEAD_DIM), F32))
    _, l, acc = lax.fori_loop(0, nch, attn_chunk, init)
    o = acc / l
    for h in range(C_HEADS):
        o_ref[:, h * HEAD_DIM:(h + 1) * HEAD_DIM] = o[h].astype(o_ref.dtype)


def _attn_c(proj, tail, tab128, tab64, cqg, ckg, ikg, *, b, s):
    tq, tk = 128, 256
    nq = s // tq
    n_sel = min(DSA_TOPK, s // 4)
    kern = functools.partial(_attn_c_kernel, tq=tq, tk=tk, n_sel=n_sel)
    small = lambda shape: pl.BlockSpec(shape, lambda bi, qt: (0,) * len(shape))
    qspec = lambda h: pl.BlockSpec((tq, LANES), lambda bi, qt: (bi * nq + qt, QC_C + h))
    return pl.pallas_call(
        kern,
        grid=(b, nq),
        in_specs=[
            qspec(0), qspec(1), qspec(2), qspec(3), qspec(4),
            pl.BlockSpec((tq, IDX_HEADS * IDX_DIM), lambda bi, qt: (bi * nq + qt, QI_C // 8)),
            pl.BlockSpec((s, LANES), lambda bi, qt: (bi, KC_C)),
            pl.BlockSpec((s, LANES), lambda bi, qt: (bi, VC_C)),
            pl.BlockSpec((s, LANES), lambda bi, qt: (bi, 0)),
            pl.BlockSpec((tq, LANES), lambda bi, qt: (bi * nq + qt, 0)),
            pl.BlockSpec((3, tq, LANES), lambda bi, qt: (0, qt, 0)),
            pl.BlockSpec((3, s, LANES), lambda bi, qt: (0, 0, 0)),
            pl.BlockSpec((3, tq, LANES), lambda bi, qt: (0, qt, 0)),
            pl.BlockSpec((3, s, LANES), lambda bi, qt: (0, 0, 0)),
            small((1, LANES)), small((1, LANES)), small((1, LANES)),
        ],
        out_specs=pl.BlockSpec((tq, C_HEADS * HEAD_DIM), lambda bi, qt: (bi * nq + qt, 0)),
        out_shape=jax.ShapeDtypeStruct((b * s, C_HEADS * HEAD_DIM), BF16),
        scratch_shapes=[pltpu.VMEM((s, LANES), BF16), pltpu.VMEM((s, LANES), BF16),
                        pltpu.VMEM((s // tk, tq, tk), I32)],
        compiler_params=_cp(("parallel", "arbitrary")),
        name="attn_dsa",
    )(proj, proj, proj, proj, proj, proj, proj, proj, tail, tail,
      tab128, tab128, tab64, tab64, cqg, ckg, ikg)


def _out_router_kernel(x_ref, oa_ref, ob_ref, oc_ref, w_ref, g2_ref, wr_ref, br_ref,
                       x1_ref, t_ref, eid_ref, gate_ref):
    na = A_HEADS * HEAD_DIM
    nb = na + B_HEADS * HEAD_DIM
    x1 = (x_ref[...]
          + jnp.dot(oa_ref[...], w_ref[0:na, :], preferred_element_type=F32)
          + jnp.dot(ob_ref[...], w_ref[na:nb, :], preferred_element_type=F32)
          + jnp.dot(oc_ref[...], w_ref[nb:, :], preferred_element_type=F32))
    x1_ref[...] = x1
    t = _rms_full(x1, g2_ref[...])
    t_ref[...] = t

    lg = jnp.dot(t, wr_ref[...], preferred_element_type=F32, precision=lax.Precision.HIGHEST) + br_ref[...]
    lane = lax.broadcasted_iota(I32, lg.shape, 1)
    ninf = -jnp.inf
    gl = jnp.where(lane < N_GROUPS, lg, ninf)
    gm = jnp.max(gl, axis=-1, keepdims=True)
    ge = jnp.exp(gl - gm)
    g_prob = ge / jnp.sum(ge, axis=-1, keepdims=True)
    g_idx = jnp.min(jnp.where(gl == gm, lane, LANES), axis=-1, keepdims=True)
    g_w = jnp.sum(jnp.where(lane == g_idx, g_prob, 0.0), axis=-1, keepdims=True)

    e0 = N_GROUPS + g_idx * EXPERTS_PER_GROUP
    emask = jnp.logical_and(lane >= e0, lane < e0 + EXPERTS_PER_GROUP)
    el = jnp.where(emask, lg, ninf)
    em = jnp.max(el, axis=-1, keepdims=True)
    ee = jnp.exp(el - em)
    ep = jnp.where(emask, ee / jnp.sum(ee, axis=-1, keepdims=True), ninf)
    v1 = jnp.max(ep, axis=-1, keepdims=True)
    i1 = jnp.min(jnp.where(ep == v1, lane, LANES), axis=-1, keepdims=True)
    ep2 = jnp.where(lane == i1, ninf, ep)
    v2 = jnp.max(ep2, axis=-1, keepdims=True)
    i2 = jnp.min(jnp.where(ep2 == v2, lane, LANES), axis=-1, keepdims=True)
    den = v1 + v2
    eid_ref[...] = jnp.where(lane == 0, i1 - N_GROUPS, jnp.where(lane == 1, i2 - N_GROUPS, 0))
    gate_ref[...] = jnp.where(lane == 0, g_w * (v1 / den), jnp.where(lane == 1, g_w * (v2 / den), 0.0))


def _out_router(x, oa, ob, oc, w_out_bf, g2, w_router, b_router):
    t, d = x.shape
    tm = 256
    row = lambda c: pl.BlockSpec((tm, c), lambda i: (i, 0))
    full = lambda r, c: pl.BlockSpec((r, c), lambda i: (0, 0))
    return pl.pallas_call(
        _out_router_kernel,
        grid=(t // tm,),
        in_specs=[row(d), row(oa.shape[1]), row(ob.shape[1]), row(oc.shape[1]),
                  full(d, d), full(1, d), full(d, LANES), full(1, LANES)],
        out_specs=[row(d), row(d), row(LANES), row(LANES)],
        out_shape=[jax.ShapeDtypeStruct((t, d), F32), jax.ShapeDtypeStruct((t, d), F32),
                   jax.ShapeDtypeStruct((t, LANES), I32), jax.ShapeDtypeStruct((t, LANES), F32)],
        compiler_params=_cp(("parallel",)),
        name="out_proj_router",
    )(x, oa, ob, oc, w_out_bf, g2.reshape(1, d), w_router, b_router)


def _moe_kernel(texp_ref, nused_ref, rtok_ref, t_hbm, rg_ref, wg_ref, wu_ref, wd_ref,
                y_ref, xbuf, sem, wg_bf, wu_bf, wd_bf, *, tm):
    i = pl.program_id(0)
    nu = nused_ref[0]

    def gather(tile, slot):
        def body(r, _):
            tok = rtok_ref[tile * tm + r]
            pltpu.make_async_copy(t_hbm.at[pl.ds(tok, 1)], xbuf.at[slot, pl.ds(r, 1)], sem.at[slot]).start()
            return 0
        lax.fori_loop(0, tm, body, 0)

    @pl.when(i == 0)
    def _():
        gather(0, 0)

    @pl.when(i + 1 < nu)
    def _():
        gather(i + 1, (i + 1) % 2)

    @pl.when(i < nu)
    def _():
        slot = i % 2
        pltpu.make_async_copy(xbuf.at[slot], xbuf.at[slot], sem.at[slot]).wait()

        changed = jnp.logical_or(i == 0, texp_ref[i] != texp_ref[jnp.maximum(i - 1, 0)])

        @pl.when(changed)
        def _():
            wg_bf[...] = wg_ref[...].astype(BF16)
            wu_bf[...] = wu_ref[...].astype(BF16)
            wd_bf[...] = wd_ref[...].astype(BF16)

        x = xbuf[slot].astype(BF16)
        g = jnp.dot(x, wg_bf[...], preferred_element_type=F32)
        u = jnp.dot(x, wu_bf[...], preferred_element_type=F32)
        h = (g * jax.nn.sigmoid(g)) * u * rg_ref[...]
        y_ref[...] = jnp.dot(h.astype(BF16), wd_bf[...], preferred_element_type=F32)

    @pl.when(i >= nu)
    def _():
        y_ref[...] = jnp.zeros_like(y_ref)


def _moe(t, tile_expert, n_used, row_token, row_gate, w_gate, w_up, w_down, *, layer, n_tiles):
    tm = MOE_TILE
    d = t.shape[1]
    grid_spec = pltpu.PrefetchScalarGridSpec(
        num_scalar_prefetch=3,
        grid=(n_tiles,),
        in_specs=[
            pl.BlockSpec(memory_space=pl.ANY),
            pl.BlockSpec((tm, 1), lambda i, te, nu, rt: (i, 0)),
            pl.BlockSpec((None, None, d, EXPERT_FF), lambda i, te, nu, rt: (layer, te[i], 0, 0)),
            pl.BlockSpec((None, None, d, EXPERT_FF), lambda i, te, nu, rt: (layer, te[i], 0, 0)),
            pl.BlockSpec((None, None, EXPERT_FF, d), lambda i, te, nu, rt: (layer, te[i], 0, 0)),
        ],
        out_specs=pl.BlockSpec((tm, d), lambda i, te, nu, rt: (i, 0)),
        scratch_shapes=[pltpu.VMEM((2, tm, d), F32), pltpu.SemaphoreType.DMA((2,)),
                        pltpu.VMEM((d, EXPERT_FF), BF16), pltpu.VMEM((d, EXPERT_FF), BF16),
                        pltpu.VMEM((EXPERT_FF, d), BF16)],
    )
    return pl.pallas_call(
        functools.partial(_moe_kernel, tm=tm),
        grid_spec=grid_spec,
        out_shape=jax.ShapeDtypeStruct((n_tiles * tm, d), F32),
        compiler_params=_cp(("arbitrary",)),
        name="moe_experts",
    )(tile_expert, n_used, row_token, t, row_gate, w_gate, w_up, w_down)


def _combine_kernel(pos_ref, x1_ref, y_hbm, o_ref, buf, sem, *, tc):
    i = pl.program_id(0)
    n = pl.num_programs(0)

    def gather(tile, slot):
        def body(r, _):
            a = (tile * tc + r) * 2
            pltpu.make_async_copy(y_hbm.at[pl.ds(pos_ref[a], 1)], buf.at[slot, 0, pl.ds(r, 1)], sem.at[slot]).start()
            pltpu.make_async_copy(y_hbm.at[pl.ds(pos_ref[a + 1], 1)], buf.at[slot, 1, pl.ds(r, 1)], sem.at[slot]).start()
            return 0
        lax.fori_loop(0, tc, body, 0)

    @pl.when(i == 0)
    def _():
        gather(0, 0)

    @pl.when(i + 1 < n)
    def _():
        gather(i + 1, (i + 1) % 2)

    slot = i % 2
    pltpu.make_async_copy(buf.at[slot], buf.at[slot], sem.at[slot]).wait()
    o_ref[...] = x1_ref[...] + buf[slot, 0] + buf[slot, 1]


def _combine(pos, x1, y):
    t, d = x1.shape
    tc = 256
    grid_spec = pltpu.PrefetchScalarGridSpec(
        num_scalar_prefetch=1,
        grid=(t // tc,),
        in_specs=[pl.BlockSpec((tc, d), lambda i, p: (i, 0)), pl.BlockSpec(memory_space=pl.ANY)],
        out_specs=pl.BlockSpec((tc, d), lambda i, p: (i, 0)),
        scratch_shapes=[pltpu.VMEM((2, 2, tc, d), F32), pltpu.SemaphoreType.DMA((2,))],
    )
    return pl.pallas_call(
        functools.partial(_combine_kernel, tc=tc),
        grid_spec=grid_spec,
        out_shape=jax.ShapeDtypeStruct((t, d), F32),
        compiler_params=_cp(("arbitrary",)),
        name="moe_combine",
    )(pos, x1, y)


def _rope_lane_tables(seq, dim):
    rot = dim // ROPE_FRAC
    half = rot // 2
    inv = 1.0 / (ROPE_THETA ** (jnp.arange(0, rot, 2, dtype=F32) / rot))
    ang = jnp.arange(seq, dtype=F32)[:, None] * inv[None, :]
    cos, sin = jnp.cos(ang), jnp.sin(ang)
    z_half = jnp.zeros((seq, half), F32)
    z_rest = jnp.zeros((seq, dim - rot), F32)
    c = jnp.concatenate([cos, cos, jnp.ones((seq, dim - rot), F32)], axis=-1)
    sa = jnp.concatenate([-sin, z_half, z_rest], axis=-1)
    sb = jnp.concatenate([z_half, sin, z_rest], axis=-1)
    reps = LANES // dim
    return jnp.stack([jnp.tile(c, (1, reps)), jnp.tile(sa, (1, reps)), jnp.tile(sb, (1, reps))])


def _routing_plan(eid, gate, n_tiles):
    tm = MOE_TILE
    e = eid[:, :2].reshape(-1)
    g = gate[:, :2].reshape(-1)
    n_assign = e.shape[0]
    onehot = (e[:, None] == jnp.arange(N_EXPERTS, dtype=I32)[None, :]).astype(I32)
    csum = jnp.cumsum(onehot, axis=0)
    rank = jnp.take_along_axis(csum, e[:, None], axis=1)[:, 0] - 1
    counts = csum[-1]
    tiles_per = (counts + tm - 1) // tm
    tile_end = jnp.cumsum(tiles_per)
    tile_start = tile_end - tiles_per
    n_used = tile_end[-1]
    pos = tile_start[e] * tm + rank
    tile_ids = jnp.minimum(jnp.arange(n_tiles, dtype=I32), n_used - 1)
    tile_expert = jnp.searchsorted(tile_end, tile_ids, side="right").astype(I32)
    row_token = jnp.zeros((n_tiles * tm,), I32).at[pos].set(jnp.arange(n_assign, dtype=I32) // 2)
    row_gate = jnp.zeros((n_tiles * tm,), F32).at[pos].set(g)
    return tile_expert, n_used.reshape(1).astype(I32), row_token, row_gate.reshape(-1, 1), pos.astype(I32)


def kernel(x, norm1_g, w_in, a_qn_g, a_kn_g, a_lambda, a_subln_g, b_qn_g, b_kn_g, c_qn_g, c_kn_g,
           idx_kn_g, w_out, norm2_g, w_group, b_group, w_expert, b_expert, w_gate, w_up, w_down):
    b, s, d = x.shape
    depth = w_in.shape[0]
    assert d == D_MODEL and s % MOBA_BLOCK == 0
    t = b * s
    n_tiles = (2 * t) // MOE_TILE + N_EXPERTS
    tab64 = _rope_lane_tables(s, A_QK)
    tab128 = _rope_lane_tables(s, HEAD_DIM)
    tile2 = lambda v: jnp.tile(v, 2).reshape(1, LANES)
    row = lambda v: v.reshape(1, LANES)

    xf = x.reshape(t, d)
    for l in range(depth):
        lam_init = 0.8 - 0.6 * math.exp(-0.3 * l)
        h = _rmsnorm_bf16(xf, norm1_g[l])
        w_main = _cast_bf16(w_in, l, MAIN_COLS)
        w_tail = jnp.pad(w_in[l, :, MAIN_COLS:], ((0, 0), (0, LANES - TAIL_COLS))).astype(BF16)
        proj = _matmul(h, w_main, BF16, 1024, 1024, "in_proj")
        tail = _matmul(h, w_tail, F32, 1024, LANES, "in_proj_tail")

        oa = _attn_a(proj, tab64, tile2(a_qn_g[l]), tile2(a_kn_g[l]), a_lambda[l], row(a_subln_g[l]),
                     b=b, s=s, lam_init=lam_init)
        ob = _attn_b(proj, tab128, row(b_qn_g[l]), row(b_kn_g[l]), b=b, s=s)
        ikg = jnp.pad(idx_kn_g[l], (0, LANES - IDX_DIM)).reshape(1, LANES)
        oc = _attn_c(proj, tail, tab128, tab64, row(c_qn_g[l]), row(c_kn_g[l]), ikg, b=b, s=s)

        w_router = jnp.pad(jnp.concatenate([w_group[l], w_expert[l]], axis=1),
                           ((0, 0), (0, LANES - N_GROUPS - N_EXPERTS)))
        b_router = jnp.pad(jnp.concatenate([b_group[l], b_expert[l]]),
                           (0, LANES - N_GROUPS - N_EXPERTS)).reshape(1, LANES)
        x1, tn, eid, gate = _out_router(xf, oa, ob, oc, _cast_bf16(w_out, l, d), norm2_g[l], w_router, b_router)

        tile_expert, n_used, row_token, row_gate, pos = _routing_plan(eid, gate, n_tiles)
        y = _moe(tn, tile_expert, n_used, row_token, row_gate, w_gate, w_up, w_down, layer=l, n_tiles=n_tiles)
        xf = _combine(pos, x1, y)
    return xf.reshape(b, s, d)
```

```python
import functools
import math

import jax
import jax.numpy as jnp
from jax import lax
from jax.experimental import pallas as pl
from jax.experimental.pallas import tpu as pltpu

F32 = jnp.float32
BF16 = jnp.bfloat16
I32 = jnp.int32

D_MODEL = 2048
HEAD_DIM = 128
A_HEADS = 6
A_QK = 64
B_HEADS = 5
MOBA_BLOCK = 256
MOBA_TOPK = 3
C_HEADS = 5
IDX_HEADS = 16
IDX_DIM = 64
DSA_TOPK = 256
ROPE_THETA = 500000.0
ROPE_FRAC = 4
EPS = 1e-6
N_GROUPS = 4
EXPERTS_PER_GROUP = 8
N_EXPERTS = N_GROUPS * EXPERTS_PER_GROUP
EXPERT_FF = 512

LANES = 128
MAIN_COLS = 6144
TAIL_COLS = 80
QA_C, KA_C, VA_C = 0, 6, 12
QB_C, KB_C, VB_C = 18, 23, 28
QC_C, KC_C, VC_C = 33, 38, 39
QI_C = 40
NEG = -1e30
LOG2E = 1.4426950408889634
INT_MIN = -(2 ** 31)
MOE_TILE = 256
VMEM_LIMIT = 56 * 1024 * 1024

_NT = (((1,), (1,)), ((), ()))


def _cp(sem, vmem=VMEM_LIMIT):
    return pltpu.CompilerParams(dimension_semantics=sem, vmem_limit_bytes=vmem)


def _rms_full(x, g):
    ms = jnp.mean(x * x, axis=-1, keepdims=True)
    return x * lax.rsqrt(ms + EPS) * g


def _rms_halves(x, g):
    lane = lax.broadcasted_iota(I32, x.shape, 1)
    lo = lane < 64
    x2 = x * x
    s_lo = jnp.sum(jnp.where(lo, x2, 0.0), axis=-1, keepdims=True)
    s_hi = jnp.sum(jnp.where(lo, 0.0, x2), axis=-1, keepdims=True)
    r = jnp.where(lo, lax.rsqrt(s_lo * (1.0 / 64) + EPS), lax.rsqrt(s_hi * (1.0 / 64) + EPS))
    return x * r * g


def _rope(x, tab_ref, half):
    c = tab_ref[0]
    sa = tab_ref[1]
    sb = tab_ref[2]
    return x * c + pltpu.roll(x, LANES - half, 1) * sa + pltpu.roll(x, half, 1) * sb


def _ones_col(rows):
    lane = lax.broadcasted_iota(I32, (rows, LANES), 1)
    return jnp.where(lane == 0, 1.0, 0.0).astype(BF16)


def _lane_max2(mrun, s):
    return jnp.maximum(mrun, jnp.maximum(s[:, :LANES], s[:, LANES:]))


def _softmax_pv(s_ref, va_ref, m, nch, rows, tk):
    acc = jnp.zeros((rows, 2 * LANES), F32)
    for c in range(nch):
        p = jnp.exp2(s_ref[c] - m).astype(BF16)
        acc = acc + jnp.dot(p, va_ref[c * tk:(c + 1) * tk, :], preferred_element_type=F32)
    return acc[:, :LANES] / acc[:, LANES:LANES + 1]


def _for_each_tile(qt, nq, body):
    for j in range(nq):
        pl.when(qt == j)(functools.partial(body, j))


def _norm_kernel(x_ref, g_ref, o_ref):
    o_ref[...] = _rms_full(x_ref[...], g_ref[...]).astype(o_ref.dtype)


def _rmsnorm_bf16(x, g):
    t, d = x.shape
    tm = 512
    return pl.pallas_call(
        _norm_kernel,
        grid=(t // tm,),
        in_specs=[pl.BlockSpec((tm, d), lambda i: (i, 0)), pl.BlockSpec((1, d), lambda i: (0, 0))],
        out_specs=pl.BlockSpec((tm, d), lambda i: (i, 0)),
        out_shape=jax.ShapeDtypeStruct((t, d), BF16),
        compiler_params=_cp(("parallel",)),
        name="rmsnorm",
    )(x, g.reshape(1, d))


def _cast_kernel(x_ref, o_ref):
    o_ref[...] = x_ref[...].astype(o_ref.dtype)


def _cast_bf16(w, layer, ncols):
    r = w.shape[1]
    tr, tc = 256, min(ncols, 1024)
    return pl.pallas_call(
        _cast_kernel,
        grid=(r // tr, ncols // tc),
        in_specs=[pl.BlockSpec((None, tr, tc), lambda i, j: (layer, i, j))],
        out_specs=pl.BlockSpec((tr, tc), lambda i, j: (i, j)),
        out_shape=jax.ShapeDtypeStruct((r, ncols), BF16),
        compiler_params=_cp(("parallel", "parallel")),
        name="cast_bf16",
    )(w)


def _matmul_kernel(x_ref, w_ref, o_ref):
    w = w_ref[...].astype(x_ref.dtype)
    o_ref[...] = jnp.dot(x_ref[...], w, preferred_element_type=F32).astype(o_ref.dtype)


def _matmul(x, w, out_dtype, tm, tn, name):
    m, k = x.shape
    n = w.shape[1]
    return pl.pallas_call(
        _matmul_kernel,
        grid=(n // tn, m // tm),
        in_specs=[pl.BlockSpec((tm, k), lambda j, i: (i, 0)), pl.BlockSpec((k, tn), lambda j, i: (0, j))],
        out_specs=pl.BlockSpec((tm, tn), lambda j, i: (i, j)),
        out_shape=jax.ShapeDtypeStruct((m, n), out_dtype),
        compiler_params=_cp(("parallel", "parallel")),
        name=name,
    )(x, w)


def _attn_a_kernel(q_ref, k_ref, v_ref, tabq_ref, tabk_ref, qg_ref, kg_ref, lam_ref, sg_ref,
                   o_ref, ks_ref, va_ref, s_ref, *, tq, nq, lam_init):
    qt = pl.program_id(2)

    @pl.when(qt == 0)
    def _():
        k = _rms_halves(k_ref[...].astype(F32), kg_ref[...])
        ks_ref[...] = _rope(k, tabk_ref, 8).astype(BF16)
        va_ref[:, :LANES] = v_ref[...]
        va_ref[:, LANES:] = _ones_col(v_ref.shape[0])

    q = _rms_halves(q_ref[...].astype(F32), qg_ref[...])
    q = _rope(q, tabq_ref, 8) * (A_QK ** -0.5 * LOG2E)
    lane = lax.broadcasted_iota(I32, q.shape, 1)
    q12 = jnp.concatenate([jnp.where(lane < 64, q, 0.0), jnp.where(lane < 64, 0.0, q)], axis=0).astype(BF16)

    lp = lam_ref[...]
    lam = (jnp.exp(jnp.sum(lp[0:1] * lp[1:2], axis=-1, keepdims=True))
           - jnp.exp(jnp.sum(lp[2:3] * lp[3:4], axis=-1, keepdims=True)) + lam_init)

    def tile(j):
        mrun = jnp.full((2 * tq, LANES), NEG, F32)
        for c in range(j + 1):
            s = lax.dot_general(q12, ks_ref[c * tq:(c + 1) * tq, :], _NT, preferred_element_type=F32)
            if c == j:
                row = lax.broadcasted_iota(I32, (2 * tq, tq), 0)
                row = jnp.where(row >= tq, row - tq, row)
                s = jnp.where(lax.broadcasted_iota(I32, (2 * tq, tq), 1) <= row, s, NEG)
            s_ref[c] = s
            mrun = _lane_max2(mrun, s)
        m = jnp.max(mrun, axis=-1, keepdims=True)
        o = _softmax_pv(s_ref, va_ref, m, j + 1, 2 * tq, tq)
        o = o[:tq] - lam * o[tq:]
        o = _rms_full(o, sg_ref[...]) * (1.0 - lam_init)
        o_ref[...] = o.astype(o_ref.dtype)

    _for_each_tile(qt, nq, tile)


def _attn_a(proj, tab64, qg, kg, lam_p, sg, *, b, s, lam_init):
    tq = 256
    nq = s // tq
    kern = functools.partial(_attn_a_kernel, tq=tq, nq=nq, lam_init=lam_init)
    small = lambda shape: pl.BlockSpec(shape, lambda bi, h, qt: (0,) * len(shape))
    return pl.pallas_call(
        kern,
        grid=(b, A_HEADS, nq),
        in_specs=[
            pl.BlockSpec((tq, LANES), lambda bi, h, qt: (bi * nq + qt, QA_C + h)),
            pl.BlockSpec((s, LANES), lambda bi, h, qt: (bi, KA_C + h)),
            pl.BlockSpec((s, LANES), lambda bi, h, qt: (bi, VA_C + h)),
            pl.BlockSpec((3, tq, LANES), lambda bi, h, qt: (0, qt, 0)),
            pl.BlockSpec((3, s, LANES), lambda bi, h, qt: (0, 0, 0)),
            small((1, LANES)), small((1, LANES)), small((4, A_QK)), small((1, LANES)),
        ],
        out_specs=pl.BlockSpec((tq, LANES), lambda bi, h, qt: (bi * nq + qt, h)),
        out_shape=jax.ShapeDtypeStruct((b * s, A_HEADS * HEAD_DIM), BF16),
        scratch_shapes=[pltpu.VMEM((s, LANES), BF16), pltpu.VMEM((s, 2 * LANES), BF16),
                        pltpu.VMEM((nq, 2 * tq, tq), F32)],
        compiler_params=_cp(("parallel", "parallel", "arbitrary")),
        name="attn_diff",
    )(proj, proj, proj, tab64, tab64, qg, kg, lam_p, sg)


def _attn_b_kernel(*refs, tq, nblk):
    nh = B_HEADS
    q_refs, k_refs, v_refs = refs[0:nh], refs[nh:2 * nh], refs[2 * nh:3 * nh]
    tabq_ref, tabk_ref, qg_ref, kg_ref, o_ref, ksa_ref, va_ref, kmean_ref, s_ref = refs[3 * nh:]
    qt = pl.program_id(1)
    s_len = k_refs[0].shape[0]

    @pl.when(qt == 0)
    def _():
        blk = lax.shift_right_logical(lax.broadcasted_iota(I32, (s_len, LANES), 0), MOBA_BLOCK.bit_length() - 1)
        onehot = jnp.where(lax.broadcasted_iota(I32, (s_len, LANES), 1) == blk, 1.0, 0.0).astype(BF16)
        ones = _ones_col(s_len)
        for h in range(nh):
            k = _rope(_rms_full(k_refs[h][...].astype(F32), kg_ref[...]), tabk_ref, 16)
            ksa_ref[h, :, :LANES] = k.astype(BF16)
            ksa_ref[h, :, LANES:] = onehot
            kmean_ref[h] = jnp.zeros((LANES, LANES), F32)
            kmean_ref[h, 0:nblk, :] = jnp.mean(k.reshape(nblk, MOBA_BLOCK, HEAD_DIM), axis=1)
            va_ref[h, :, :LANES] = v_refs[h][...]
            va_ref[h, :, LANES:] = ones

    lane = lax.broadcasted_iota(I32, (tq, LANES), 1)
    q_aug = []
    for h in range(nh):
        q = _rope(_rms_full(q_refs[h][...].astype(F32), qg_ref[...]), tabq_ref, 16)
        gate = lax.dot_general(q, kmean_ref[h], _NT, preferred_element_type=F32,
                               precision=lax.Precision.HIGHEST)
        g = jnp.where(lane < qt, gate, -jnp.inf)
        keep = lane == qt
        for _ in range(MOBA_TOPK):
            mx = jnp.max(g, axis=-1, keepdims=True)
            first = jnp.min(jnp.where(g == mx, lane, LANES), axis=-1, keepdims=True)
            pick = jnp.logical_and(lane == first, mx > -jnp.inf)
            keep = jnp.logical_or(keep, pick)
            g = jnp.where(pick, -jnp.inf, g)
        bias = jnp.where(keep, 0.0, NEG)
        q_aug.append(jnp.concatenate([q * (HEAD_DIM ** -0.5 * LOG2E), bias], axis=1).astype(BF16))

    def tile(j):
        causal = lax.broadcasted_iota(I32, (tq, tq), 1) <= lax.broadcasted_iota(I32, (tq, tq), 0)
        for h in range(nh):
            mrun = jnp.full((tq, LANES), NEG, F32)
            for c in range(j + 1):
                s = lax.dot_general(q_aug[h], ksa_ref[h, c * tq:(c + 1) * tq, :], _NT,
                                    preferred_element_type=F32)
                if c == j:
                    s = jnp.where(causal, s, NEG)
                s_ref[h, c] = s
                mrun = _lane_max2(mrun, s)
            m = jnp.max(mrun, axis=-1, keepdims=True)
            o = _softmax_pv(s_ref.at[h], va_ref.at[h], m, j + 1, tq, tq)
            o_ref[:, h * HEAD_DIM:(h + 1) * HEAD_DIM] = o.astype(o_ref.dtype)

    _for_each_tile(qt, nblk, tile)


def _attn_b(proj, tab128, qg, kg, *, b, s):
    tq = MOBA_BLOCK
    nq = s // tq
    nh = B_HEADS
    kern = functools.partial(_attn_b_kernel, tq=tq, nblk=nq)
    small = lambda shape: pl.BlockSpec(shape, lambda bi, qt: (0,) * len(shape))
    qspec = lambda h: pl.BlockSpec((tq, LANES), lambda bi, qt: (bi * nq + qt, QB_C + h))
    kspec = lambda c0, h: pl.BlockSpec((s, LANES), lambda bi, qt: (bi, c0 + h))
    return pl.pallas_call(
        kern,
        grid=(b, nq),
        in_specs=([qspec(h) for h in range(nh)] + [kspec(KB_C, h) for h in range(nh)]
                  + [kspec(VB_C, h) for h in range(nh)]
                  + [pl.BlockSpec((3, tq, LANES), lambda bi, qt: (0, qt, 0)),
                     pl.BlockSpec((3, s, LANES), lambda bi, qt: (0, 0, 0)),
                     small((1, LANES)), small((1, LANES))]),
        out_specs=pl.BlockSpec((tq, nh * HEAD_DIM), lambda bi, qt: (bi * nq + qt, 0)),
        out_shape=jax.ShapeDtypeStruct((b * s, nh * HEAD_DIM), BF16),
        scratch_shapes=[pltpu.VMEM((nh, s, 2 * LANES), BF16), pltpu.VMEM((nh, s, 2 * LANES), BF16),
                        pltpu.VMEM((nh, LANES, LANES), F32), pltpu.VMEM((nh, nq, tq, tq), F32)],
        compiler_params=_cp(("parallel", "arbitrary")),
        name="attn_moba",
    )(*([proj] * (3 * nh)), tab128, tab128, qg, kg)


def _attn_c_kernel(q0_ref, q1_ref, q2_ref, q3_ref, q4_ref, qi_ref, kc_ref, vc_ref, kt_ref, wt_ref,
                   t128q_ref, t128k_ref, t64q_ref, t64k_ref, cqg_ref, ckg_ref, ikg_ref,
                   o_ref, kcs_ref, kis_ref, va_ref, keys_ref, s_ref, *, tq, tk, nq, n_sel):
    qt = pl.program_id(1)

    @pl.when(qt == 0)
    def _():
        k = _rope(_rms_full(kc_ref[...].astype(F32), ckg_ref[...]), t128k_ref, 16)
        kcs_ref[...] = k.astype(BF16)
        va_ref[:, :LANES] = vc_ref[...]
        va_ref[:, LANES:] = _ones_col(vc_ref.shape[0])
        t = kt_ref[...]
        lane = lax.broadcasted_iota(I32, t.shape, 1)
        ms = jnp.sum(jnp.where(lane < IDX_DIM, t * t, 0.0), axis=-1, keepdims=True) * (1.0 / IDX_DIM)
        ki = _rope(t * lax.rsqrt(ms + EPS) * ikg_ref[...], t64k_ref, 8)
        kis_ref[...] = (ki + pltpu.roll(ki, 64, 1)).astype(BF16)

    lane = lax.broadcasted_iota(I32, (tq, LANES), 1)
    heads = []
    for j in range(IDX_HEADS // 2):
        x = _rope(qi_ref[:, j * LANES:(j + 1) * LANES].astype(F32), t64q_ref, 8)
        heads.append(jnp.where(lane < 64, x, 0.0).astype(BF16))
        heads.append(jnp.where(lane < 64, 0.0, x).astype(BF16))
    hg = 4
    qi_groups = [jnp.concatenate(heads[g:g + hg], axis=0) for g in range(0, IDX_HEADS, hg)]
    wt = wt_ref[...] * ((IDX_HEADS ** -0.5) * (IDX_DIM ** -0.5))
    w_cols = [wt[:, IDX_DIM + h:IDX_DIM + h + 1] for h in range(IDX_HEADS)]

    causal = lax.broadcasted_iota(I32, (tq, tk), 1) <= lax.broadcasted_iota(I32, (tq, tk), 0)

    def index_keys(j):
        for c in range(j + 1):
            kchunk = kis_ref[c * tk:(c + 1) * tk, :]
            sc = jnp.zeros((tq, tk), F32)
            for g, qg in enumerate(qi_groups):
                lg = lax.dot_general(qg, kchunk, _NT, preferred_element_type=F32)
                for i in range(hg):
                    sc = sc + jnp.maximum(lg[i * tq:(i + 1) * tq], 0.0) * w_cols[g * hg + i]
            bits = pltpu.bitcast(sc, I32)
            key = jnp.where(bits < 0, bits ^ jnp.int32(0x7FFFFFFF), bits)
            if c == j:
                key = jnp.where(causal, key, jnp.int32(INT_MIN))
            keys_ref[c] = key

    def threshold(j):
        def count_ge(cand):
            tot = jnp.zeros((tq, tk), F32)
            for c in range(j + 1):
                tot = tot + jnp.where(keys_ref[c] >= cand, 1.0, 0.0)
            return jnp.sum(tot, axis=-1, keepdims=True)

        lo = jnp.where(count_ge(jnp.zeros((tq, 1), I32)) >= n_sel, jnp.int32(0), jnp.int32(INT_MIN))

        def bit_step(i, lo):
            cand = lo + lax.shift_left(jnp.int32(1), 30 - i)
            return jnp.where(count_ge(cand) >= n_sel, cand, lo)

        lo = lax.fori_loop(0, 31, bit_step, lo)
        return jnp.maximum(lo, jnp.int32(INT_MIN + 1))

    qs = []
    for q_ref in (q0_ref, q1_ref, q2_ref, q3_ref, q4_ref):
        q = _rope(_rms_full(q_ref[...].astype(F32), cqg_ref[...]), t128q_ref, 16)
        qs.append((q * (HEAD_DIM ** -0.5 * LOG2E)).astype(BF16))
    q_all = jnp.concatenate(qs, axis=0)
    rows = C_HEADS * tq

    def tile(j):
        index_keys(j)
        thr = threshold(j)
        mrun = jnp.full((rows, LANES), NEG, F32)
        for c in range(j + 1):
            s = lax.dot_general(q_all, kcs_ref[c * tk:(c + 1) * tk, :], _NT, preferred_element_type=F32)
            s = jnp.where((keys_ref[c] >= thr)[None], s.reshape(C_HEADS, tq, tk), NEG).reshape(rows, tk)
            s_ref[c] = s
            mrun = _lane_max2(mrun, s)
        m = jnp.max(mrun, axis=-1, keepdims=True)
        o = _softmax_pv(s_ref, va_ref, m, j + 1, rows, tk)
        for h in range(C_HEADS):
            o_ref[:, h * HEAD_DIM:(h + 1) * HEAD_DIM] = o[h * tq:(h + 1) * tq].astype(o_ref.dtype)

    _for_each_tile(qt, nq, tile)


def _attn_c(proj, tail, tab128, tab64, cqg, ckg, ikg, *, b, s):
    tq, tk = 256, 256
    nq = s // tq
    n_sel = min(DSA_TOPK, s // 4)
    assert tq == tk
    kern = functools.partial(_attn_c_kernel, tq=tq, tk=tk, nq=nq, n_sel=n_sel)
    small = lambda shape: pl.BlockSpec(shape, lambda bi, qt: (0,) * len(shape))
    qspec = lambda h: pl.BlockSpec((tq, LANES), lambda bi, qt: (bi * nq + qt, QC_C + h))
    return pl.pallas_call(
        kern,
        grid=(b, nq),
        in_specs=[
            qspec(0), qspec(1), qspec(2), qspec(3), qspec(4),
            pl.BlockSpec((tq, IDX_HEADS * IDX_DIM), lambda bi, qt: (bi * nq + qt, QI_C // 8)),
            pl.BlockSpec((s, LANES), lambda bi, qt: (bi, KC_C)),
            pl.BlockSpec((s, LANES), lambda bi, qt: (bi, VC_C)),
            pl.BlockSpec((s, LANES), lambda bi, qt: (bi, 0)),
            pl.BlockSpec((tq, LANES), lambda bi, qt: (bi * nq + qt, 0)),
            pl.BlockSpec((3, tq, LANES), lambda bi, qt: (0, qt, 0)),
            pl.BlockSpec((3, s, LANES), lambda bi, qt: (0, 0, 0)),
            pl.BlockSpec((3, tq, LANES), lambda bi, qt: (0, qt, 0)),
            pl.BlockSpec((3, s, LANES), lambda bi, qt: (0, 0, 0)),
            small((1, LANES)), small((1, LANES)), small((1, LANES)),
        ],
        out_specs=pl.BlockSpec((tq, C_HEADS * HEAD_DIM), lambda bi, qt: (bi * nq + qt, 0)),
        out_shape=jax.ShapeDtypeStruct((b * s, C_HEADS * HEAD_DIM), BF16),
        scratch_shapes=[pltpu.VMEM((s, LANES), BF16), pltpu.VMEM((s, LANES), BF16),
                        pltpu.VMEM((s, 2 * LANES), BF16), pltpu.VMEM((s // tk, tq, tk), I32),
                        pltpu.VMEM((s // tk, C_HEADS * tq, tk), F32)],
        compiler_params=_cp(("parallel", "arbitrary")),
        name="attn_dsa",
    )(proj, proj, proj, proj, proj, proj, proj, proj, tail, tail,
      tab128, tab128, tab64, tab64, cqg, ckg, ikg)


def _out_router_kernel(x_ref, oa_ref, ob_ref, oc_ref, w_ref, g2_ref, wr_ref, br_ref,
                       x1_ref, t_ref, eid_ref, gate_ref):
    na = A_HEADS * HEAD_DIM
    nb = na + B_HEADS * HEAD_DIM
    x1 = (x_ref[...]
          + jnp.dot(oa_ref[...], w_ref[0:na, :], preferred_element_type=F32)
          + jnp.dot(ob_ref[...], w_ref[na:nb, :], preferred_element_type=F32)
          + jnp.dot(oc_ref[...], w_ref[nb:, :], preferred_element_type=F32))
    x1_ref[...] = x1
    t = _rms_full(x1, g2_ref[...])
    t_ref[...] = t

    lg = jnp.dot(t, wr_ref[...], preferred_element_type=F32, precision=lax.Precision.HIGHEST) + br_ref[...]
    lane = lax.broadcasted_iota(I32, lg.shape, 1)
    ninf = -jnp.inf
    gl = jnp.where(lane < N_GROUPS, lg, ninf)
    gm = jnp.max(gl, axis=-1, keepdims=True)
    ge = jnp.exp(gl - gm)
    g_prob = ge / jnp.sum(ge, axis=-1, keepdims=True)
    g_idx = jnp.min(jnp.where(gl == gm, lane, LANES), axis=-1, keepdims=True)
    g_w = jnp.sum(jnp.where(lane == g_idx, g_prob, 0.0), axis=-1, keepdims=True)

    e0 = N_GROUPS + g_idx * EXPERTS_PER_GROUP
    emask = jnp.logical_and(lane >= e0, lane < e0 + EXPERTS_PER_GROUP)
    el = jnp.where(emask, lg, ninf)
    em = jnp.max(el, axis=-1, keepdims=True)
    ee = jnp.exp(el - em)
    ep = jnp.where(emask, ee / jnp.sum(ee, axis=-1, keepdims=True), ninf)
    v1 = jnp.max(ep, axis=-1, keepdims=True)
    i1 = jnp.min(jnp.where(ep == v1, lane, LANES), axis=-1, keepdims=True)
    ep2 = jnp.where(lane == i1, ninf, ep)
    v2 = jnp.max(ep2, axis=-1, keepdims=True)
    i2 = jnp.min(jnp.where(ep2 == v2, lane, LANES), axis=-1, keepdims=True)
    den = v1 + v2
    eid_ref[...] = jnp.where(lane == 0, i1 - N_GROUPS, jnp.where(lane == 1, i2 - N_GROUPS, 0))
    gate_ref[...] = jnp.where(lane == 0, g_w * (v1 / den), jnp.where(lane == 1, g_w * (v2 / den), 0.0))


def _out_router(x, oa, ob, oc, w_out_bf, g2, w_router, b_router):
    t, d = x.shape
    tm = 256
    row = lambda c: pl.BlockSpec((tm, c), lambda i: (i, 0))
    full = lambda r, c: pl.BlockSpec((r, c), lambda i: (0, 0))
    return pl.pallas_call(
        _out_router_kernel,
        grid=(t // tm,),
        in_specs=[row(d), row(oa.shape[1]), row(ob.shape[1]), row(oc.shape[1]),
                  full(d, d), full(1, d), full(d, LANES), full(1, LANES)],
        out_specs=[row(d), row(d), row(LANES), row(LANES)],
        out_shape=[jax.ShapeDtypeStruct((t, d), F32), jax.ShapeDtypeStruct((t, d), F32),
                   jax.ShapeDtypeStruct((t, LANES), I32), jax.ShapeDtypeStruct((t, LANES), F32)],
        compiler_params=_cp(("parallel",)),
        name="out_proj_router",
    )(x, oa, ob, oc, w_out_bf, g2.reshape(1, d), w_router, b_router)


def _moe_kernel(texp_ref, nused_ref, rtok_ref, t_hbm, wg_ref, wu_ref, wd_ref,
                y_ref, xbuf, sem, wg_bf, wu_bf, wd_bf, *, tm):
    i = pl.program_id(0)
    nu = nused_ref[0]

    def gather(tile, slot):
        def body(r, _):
            tok = rtok_ref[tile * tm + r]
            pltpu.make_async_copy(t_hbm.at[pl.ds(tok, 1)], xbuf.at[slot, pl.ds(r, 1)], sem.at[slot]).start()
            return 0
        lax.fori_loop(0, tm, body, 0)

    @pl.when(i == 0)
    def _():
        gather(0, 0)

    @pl.when(i + 1 < nu)
    def _():
        gather(i + 1, (i + 1) % 2)

    @pl.when(i < nu)
    def _():
        slot = i % 2
        pltpu.make_async_copy(xbuf.at[slot], xbuf.at[slot], sem.at[slot]).wait()

        changed = jnp.logical_or(i == 0, texp_ref[i] != texp_ref[jnp.maximum(i - 1, 0)])

        @pl.when(changed)
        def _():
            wg_bf[...] = wg_ref[...].astype(BF16)
            wu_bf[...] = wu_ref[...].astype(BF16)
            wd_bf[...] = wd_ref[...].astype(BF16)

        x = xbuf[slot].astype(BF16)
        g = jnp.dot(x, wg_bf[...], preferred_element_type=F32)
        u = jnp.dot(x, wu_bf[...], preferred_element_type=F32)
        h = (g * jax.nn.sigmoid(g)) * u
        y_ref[...] = jnp.dot(h.astype(BF16), wd_bf[...], preferred_element_type=F32)

    @pl.when(i >= nu)
    def _():
        y_ref[...] = jnp.zeros_like(y_ref)


def _moe(t, tile_expert, n_used, row_token, w_gate, w_up, w_down, *, layer, n_tiles):
    tm = MOE_TILE
    d = t.shape[1]
    grid_spec = pltpu.PrefetchScalarGridSpec(
        num_scalar_prefetch=3,
        grid=(n_tiles,),
        in_specs=[
            pl.BlockSpec(memory_space=pl.ANY),
            pl.BlockSpec((None, None, d, EXPERT_FF), lambda i, te, nu, rt: (layer, te[i], 0, 0)),
            pl.BlockSpec((None, None, d, EXPERT_FF), lambda i, te, nu, rt: (layer, te[i], 0, 0)),
            pl.BlockSpec((None, None, EXPERT_FF, d), lambda i, te, nu, rt: (layer, te[i], 0, 0)),
        ],
        out_specs=pl.BlockSpec((tm, d), lambda i, te, nu, rt: (i, 0)),
        scratch_shapes=[pltpu.VMEM((2, tm, d), F32), pltpu.SemaphoreType.DMA((2,)),
                        pltpu.VMEM((d, EXPERT_FF), BF16), pltpu.VMEM((d, EXPERT_FF), BF16),
                        pltpu.VMEM((EXPERT_FF, d), BF16)],
    )
    return pl.pallas_call(
        functools.partial(_moe_kernel, tm=tm),
        grid_spec=grid_spec,
        out_shape=jax.ShapeDtypeStruct((n_tiles * tm, d), F32),
        compiler_params=_cp(("arbitrary",)),
        name="moe_experts",
    )(tile_expert, n_used, row_token, t, w_gate, w_up, w_down)


def _combine_kernel(pos_ref, x1_ref, gate_ref, y_hbm, o_ref, buf, sem, *, tc):
    i = pl.program_id(0)
    n = pl.num_programs(0)

    def gather(tile, slot):
        def body(r, _):
            a = (tile * tc + r) * 2
            pltpu.make_async_copy(y_hbm.at[pl.ds(pos_ref[a], 1)], buf.at[slot, 0, pl.ds(r, 1)], sem.at[slot]).start()
            pltpu.make_async_copy(y_hbm.at[pl.ds(pos_ref[a + 1], 1)], buf.at[slot, 1, pl.ds(r, 1)], sem.at[slot]).start()
            return 0
        lax.fori_loop(0, tc, body, 0)

    @pl.when(i == 0)
    def _():
        gather(0, 0)

    @pl.when(i + 1 < n)
    def _():
        gather(i + 1, (i + 1) % 2)

    slot = i % 2
    pltpu.make_async_copy(buf.at[slot], buf.at[slot], sem.at[slot]).wait()
    gt = gate_ref[...]
    o_ref[...] = x1_ref[...] + gt[:, 0:1] * buf[slot, 0] + gt[:, 1:2] * buf[slot, 1]


def _combine(pos, x1, gate, y):
    t, d = x1.shape
    tc = 256
    grid_spec = pltpu.PrefetchScalarGridSpec(
        num_scalar_prefetch=1,
        grid=(t // tc,),
        in_specs=[pl.BlockSpec((tc, d), lambda i, p: (i, 0)), pl.BlockSpec((tc, LANES), lambda i, p: (i, 0)),
                  pl.BlockSpec(memory_space=pl.ANY)],
        out_specs=pl.BlockSpec((tc, d), lambda i, p: (i, 0)),
        scratch_shapes=[pltpu.VMEM((2, 2, tc, d), F32), pltpu.SemaphoreType.DMA((2,))],
    )
    return pl.pallas_call(
        functools.partial(_combine_kernel, tc=tc),
        grid_spec=grid_spec,
        out_shape=jax.ShapeDtypeStruct((t, d), F32),
        compiler_params=_cp(("arbitrary",)),
        name="moe_combine",
    )(pos, x1, gate, y)


def _rope_lane_tables(seq, dim):
    rot = dim // ROPE_FRAC
    half = rot // 2
    inv = 1.0 / (ROPE_THETA ** (jnp.arange(0, rot, 2, dtype=F32) / rot))
    ang = jnp.arange(seq, dtype=F32)[:, None] * inv[None, :]
    cos, sin = jnp.cos(ang), jnp.sin(ang)
    z_half = jnp.zeros((seq, half), F32)
    z_rest = jnp.zeros((seq, dim - rot), F32)
    c = jnp.concatenate([cos, cos, jnp.ones((seq, dim - rot), F32)], axis=-1)
    sa = jnp.concatenate([-sin, z_half, z_rest], axis=-1)
    sb = jnp.concatenate([z_half, sin, z_rest], axis=-1)
    reps = LANES // dim
    return jnp.stack([jnp.tile(c, (1, reps)), jnp.tile(sa, (1, reps)), jnp.tile(sb, (1, reps))])


def _routing_plan(eid, n_tiles):
    tm = MOE_TILE
    e = eid[:, :2].reshape(-1)
    n_assign = e.shape[0]
    onehot = (e[:, None] == jnp.arange(N_EXPERTS, dtype=I32)[None, :]).astype(I32)
    csum = jnp.cumsum(onehot, axis=0)
    rank = jnp.take_along_axis(csum, e[:, None], axis=1)[:, 0] - 1
    counts = csum[-1]
    tiles_per = (counts + tm - 1) // tm
    tile_end = jnp.cumsum(tiles_per)
    tile_start = tile_end - tiles_per
    n_used = tile_end[-1]
    pos = tile_start[e] * tm + rank
    tile_ids = jnp.minimum(jnp.arange(n_tiles, dtype=I32), n_used - 1)
    tile_expert = jnp.sum((tile_end[None, :] <= tile_ids[:, None]).astype(I32), axis=1)
    row_token = jnp.zeros((n_tiles * tm,), I32).at[pos].set(jnp.arange(n_assign, dtype=I32) // 2)
    return tile_expert, n_used.reshape(1).astype(I32), row_token, pos.astype(I32)


def kernel(x, norm1_g, w_in, a_qn_g, a_kn_g, a_lambda, a_subln_g, b_qn_g, b_kn_g, c_qn_g, c_kn_g,
           idx_kn_g, w_out, norm2_g, w_group, b_group, w_expert, b_expert, w_gate, w_up, w_down):
    b, s, d = x.shape
    depth = w_in.shape[0]
    assert d == D_MODEL and s % MOBA_BLOCK == 0
    t = b * s
    n_tiles = (2 * t) // MOE_TILE + N_EXPERTS
    tab64 = _rope_lane_tables(s, A_QK)
    tab128 = _rope_lane_tables(s, HEAD_DIM)
    tile2 = lambda v: jnp.tile(v, 2).reshape(1, LANES)
    row = lambda v: v.reshape(1, LANES)

    xf = x.reshape(t, d)
    for l in range(depth):
        lam_init = 0.8 - 0.6 * math.exp(-0.3 * l)
        h = _rmsnorm_bf16(xf, norm1_g[l])
        w_main = _cast_bf16(w_in, l, MAIN_COLS)
        w_tail = jnp.pad(w_in[l, :, MAIN_COLS:], ((0, 0), (0, LANES - TAIL_COLS)))
        proj = _matmul(h, w_main, BF16, 1024, 1024, "in_proj")
        tail = _matmul(h, w_tail, F32, 1024, LANES, "in_proj_tail")

        oa = _attn_a(proj, tab64, tile2(a_qn_g[l]), tile2(a_kn_g[l]), a_lambda[l], row(a_subln_g[l]),
                     b=b, s=s, lam_init=lam_init)
        ob = _attn_b(proj, tab128, row(b_qn_g[l]), row(b_kn_g[l]), b=b, s=s)
        ikg = jnp.pad(idx_kn_g[l], (0, LANES - IDX_DIM)).reshape(1, LANES)
        oc = _attn_c(proj, tail, tab128, tab64, row(c_qn_g[l]), row(c_kn_g[l]), ikg, b=b, s=s)

        w_router = jnp.pad(jnp.concatenate([w_group[l], w_expert[l]], axis=1),
                           ((0, 0), (0, LANES - N_GROUPS - N_EXPERTS)))
        b_router = jnp.pad(jnp.concatenate([b_group[l], b_expert[l]]),
                           (0, LANES - N_GROUPS - N_EXPERTS)).reshape(1, LANES)
        x1, tn, eid, gate = _out_router(xf, oa, ob, oc, _cast_bf16(w_out, l, d), norm2_g[l], w_router, b_router)

        tile_expert, n_used, row_token, pos = _routing_plan(eid, n_tiles)
        y = _moe(tn, tile_expert, n_used, row_token, w_gate, w_up, w_down, layer=l, n_tiles=n_tiles)
        xf = _combine(pos, x1, gate, y)
    return xf.reshape(b, s, d)
```

```python
import functools
import math

import jax
import jax.numpy as jnp
from jax import lax
from jax.experimental import pallas as pl
from jax.experimental.pallas import tpu as pltpu

F32 = jnp.float32
BF16 = jnp.bfloat16
I32 = jnp.int32

D_MODEL = 2048
HEAD_DIM = 128
A_HEADS = 6
A_QK = 64
B_HEADS = 5
MOBA_BLOCK = 256
MOBA_TOPK = 3
C_HEADS = 5
IDX_HEADS = 16
IDX_DIM = 64
DSA_TOPK = 256
ROPE_THETA = 500000.0
ROPE_FRAC = 4
EPS = 1e-6
N_GROUPS = 4
EXPERTS_PER_GROUP = 8
N_EXPERTS = N_GROUPS * EXPERTS_PER_GROUP
EXPERT_FF = 512

LANES = 128
MAIN_COLS = 6144
TAIL_COLS = 80
QA_C, KA_C, VA_C = 0, 6, 12
QB_C, KB_C, VB_C = 18, 23, 28
QC_C, KC_C, VC_C = 33, 38, 39
QI_C = 40
NEG = -1e30
LOG2E = 1.4426950408889634
INT_MIN = -(2 ** 31)
MOE_TILE = 256
VMEM_LIMIT = 56 * 1024 * 1024

_NT = (((1,), (1,)), ((), ()))


def _cp(sem, vmem=VMEM_LIMIT):
    return pltpu.CompilerParams(dimension_semantics=sem, vmem_limit_bytes=vmem)


def _rms_full(x, g):
    ms = jnp.mean(x * x, axis=-1, keepdims=True)
    return x * lax.rsqrt(ms + EPS) * g


def _rms_halves(x, g):
    lane = lax.broadcasted_iota(I32, x.shape, 1)
    lo = lane < 64
    x2 = x * x
    s_lo = jnp.sum(jnp.where(lo, x2, 0.0), axis=-1, keepdims=True)
    s_hi = jnp.sum(jnp.where(lo, 0.0, x2), axis=-1, keepdims=True)
    r = jnp.where(lo, lax.rsqrt(s_lo * (1.0 / 64) + EPS), lax.rsqrt(s_hi * (1.0 / 64) + EPS))
    return x * r * g


def _rope(x, tab_ref, half):
    c = tab_ref[0]
    sa = tab_ref[1]
    sb = tab_ref[2]
    return x * c + pltpu.roll(x, LANES - half, 1) * sa + pltpu.roll(x, half, 1) * sb


def _ones_col(rows):
    lane = lax.broadcasted_iota(I32, (rows, LANES), 1)
    return jnp.where(lane == 0, 1.0, 0.0).astype(BF16)


def _lane_max2(mrun, s):
    return jnp.maximum(mrun, jnp.maximum(s[:, :LANES], s[:, LANES:]))


def _attend(qs, k_refs, va_refs, s_refs, qt, tk, *, diag_mask=None, mask_fn=None):
    n = len(qs)
    rows = qs[0].shape[0]

    def scores(h, c):
        off = pl.multiple_of(c * tk, tk)
        s = lax.dot_general(qs[h], k_refs[h][pl.ds(off, tk), :], _NT, preferred_element_type=F32)
        return s if mask_fn is None else mask_fn(c, s)

    def first(c, mruns):
        out = []
        for h in range(n):
            s = scores(h, c)
            s_refs[h][c] = s
            out.append(_lane_max2(mruns[h], s))
        return tuple(out)

    init = tuple(jnp.full((rows, LANES), NEG, F32) for _ in range(n))
    if diag_mask is None:
        mruns = lax.fori_loop(0, qt + 1, first, init)
    else:
        mruns = list(lax.fori_loop(0, qt, first, init))
        for h in range(n):
            s = jnp.where(diag_mask, scores(h, qt), NEG)
            s_refs[h][qt] = s
            mruns[h] = _lane_max2(mruns[h], s)
    ms = [jnp.max(mr, axis=-1, keepdims=True) for mr in mruns]

    def second(c, accs):
        off = pl.multiple_of(c * tk, tk)
        return tuple(
            accs[h] + jnp.dot(jnp.exp2(s_refs[h][c] - ms[h]).astype(BF16), va_refs[h][pl.ds(off, tk), :],
                              preferred_element_type=F32)
            for h in range(n))

    accs = lax.fori_loop(0, qt + 1, second, tuple(jnp.zeros((rows, 2 * LANES), F32) for _ in range(n)))
    return [a[:, :LANES] / a[:, LANES:LANES + 1] for a in accs]


def _norm_kernel(x_ref, g_ref, o_ref):
    o_ref[...] = _rms_full(x_ref[...], g_ref[...]).astype(o_ref.dtype)


def _rmsnorm_bf16(x, g):
    t, d = x.shape
    tm = 512
    return pl.pallas_call(
        _norm_kernel,
        grid=(t // tm,),
        in_specs=[pl.BlockSpec((tm, d), lambda i: (i, 0)), pl.BlockSpec((1, d), lambda i: (0, 0))],
        out_specs=pl.BlockSpec((tm, d), lambda i: (i, 0)),
        out_shape=jax.ShapeDtypeStruct((t, d), BF16),
        compiler_params=_cp(("parallel",)),
        name="rmsnorm",
    )(x, g.reshape(1, d))


def _cast_kernel(x_ref, o_ref):
    o_ref[...] = x_ref[...].astype(o_ref.dtype)


def _cast_bf16(w, layer, ncols):
    r = w.shape[1]
    tr, tc = 256, min(ncols, 1024)
    return pl.pallas_call(
        _cast_kernel,
        grid=(r // tr, ncols // tc),
        in_specs=[pl.BlockSpec((None, tr, tc), lambda i, j: (layer, i, j))],
        out_specs=pl.BlockSpec((tr, tc), lambda i, j: (i, j)),
        out_shape=jax.ShapeDtypeStruct((r, ncols), BF16),
        compiler_params=_cp(("parallel", "parallel")),
        name="cast_bf16",
    )(w)


def _matmul_kernel(x_ref, w_ref, o_ref):
    w = w_ref[...].astype(x_ref.dtype)
    o_ref[...] = jnp.dot(x_ref[...], w, preferred_element_type=F32).astype(o_ref.dtype)


def _matmul(x, w, out_dtype, tm, tn, name):
    m, k = x.shape
    n = w.shape[1]
    return pl.pallas_call(
        _matmul_kernel,
        grid=(n // tn, m // tm),
        in_specs=[pl.BlockSpec((tm, k), lambda j, i: (i, 0)), pl.BlockSpec((k, tn), lambda j, i: (0, j))],
        out_specs=pl.BlockSpec((tm, tn), lambda j, i: (i, j)),
        out_shape=jax.ShapeDtypeStruct((m, n), out_dtype),
        compiler_params=_cp(("parallel", "parallel")),
        name=name,
    )(x, w)


def _attn_a_kernel(*refs, tq, nh, lam_init):
    q_refs, k_refs, v_refs = refs[0:nh], refs[nh:2 * nh], refs[2 * nh:3 * nh]
    tabq_ref, tabk_ref, qg_ref, kg_ref, lam_ref, sg_ref, o_ref, ks_ref, va_ref, s_ref = refs[3 * nh:]
    qt = pl.program_id(2)

    @pl.when(qt == 0)
    def _():
        ones = _ones_col(ks_ref.shape[1])
        for h in range(nh):
            k = _rms_halves(k_refs[h][...].astype(F32), kg_ref[...])
            ks_ref[h] = _rope(k, tabk_ref, 8).astype(BF16)
            va_ref[h, :, :LANES] = v_refs[h][...]
            va_ref[h, :, LANES:] = ones

    lane = lax.broadcasted_iota(I32, (tq, LANES), 1)
    qs = []
    for h in range(nh):
        q = _rms_halves(q_refs[h][...].astype(F32), qg_ref[...])
        q = _rope(q, tabq_ref, 8) * (A_QK ** -0.5 * LOG2E)
        qs.append(jnp.concatenate([jnp.where(lane < 64, q, 0.0), jnp.where(lane < 64, 0.0, q)],
                                  axis=0).astype(BF16))

    lp = lam_ref[...]
    lam = (jnp.exp(jnp.sum(lp[0:1] * lp[1:2], axis=-1, keepdims=True))
           - jnp.exp(jnp.sum(lp[2:3] * lp[3:4], axis=-1, keepdims=True)) + lam_init)

    row = lax.broadcasted_iota(I32, (2 * tq, tq), 0)
    row = jnp.where(row >= tq, row - tq, row)
    causal = lax.broadcasted_iota(I32, (2 * tq, tq), 1) <= row
    outs = _attend(qs, [ks_ref.at[h] for h in range(nh)], [va_ref.at[h] for h in range(nh)],
                   [s_ref.at[h] for h in range(nh)], qt, tq, diag_mask=causal)
    for h in range(nh):
        o = outs[h][:tq] - lam * outs[h][tq:]
        o = _rms_full(o, sg_ref[...]) * (1.0 - lam_init)
        o_ref[:, h * HEAD_DIM:(h + 1) * HEAD_DIM] = o.astype(o_ref.dtype)


def _attn_a(proj, tab64, qg, kg, lam_p, sg, *, b, s, lam_init):
    tq = 256
    nq = s // tq
    nh = 3
    ng = A_HEADS // nh
    kern = functools.partial(_attn_a_kernel, tq=tq, nh=nh, lam_init=lam_init)
    small = lambda shape: pl.BlockSpec(shape, lambda bi, g, qt: (0,) * len(shape))
    qspec = lambda h: pl.BlockSpec((tq, LANES), lambda bi, g, qt: (bi * nq + qt, QA_C + g * nh + h))
    kspec = lambda c0, h: pl.BlockSpec((s, LANES), lambda bi, g, qt: (bi, c0 + g * nh + h))
    return pl.pallas_call(
        kern,
        grid=(b, ng, nq),
        in_specs=([qspec(h) for h in range(nh)] + [kspec(KA_C, h) for h in range(nh)]
                  + [kspec(VA_C, h) for h in range(nh)]
                  + [pl.BlockSpec((3, tq, LANES), lambda bi, g, qt: (0, qt, 0)),
                     pl.BlockSpec((3, s, LANES), lambda bi, g, qt: (0, 0, 0)),
                     small((1, LANES)), small((1, LANES)), small((4, A_QK)), small((1, LANES))]),
        out_specs=pl.BlockSpec((tq, nh * HEAD_DIM), lambda bi, g, qt: (bi * nq + qt, g)),
        out_shape=jax.ShapeDtypeStruct((b * s, A_HEADS * HEAD_DIM), BF16),
        scratch_shapes=[pltpu.VMEM((nh, s, LANES), BF16), pltpu.VMEM((nh, s, 2 * LANES), BF16),
                        pltpu.VMEM((nh, nq, 2 * tq, tq), F32)],
        compiler_params=_cp(("parallel", "parallel", "arbitrary")),
        name="attn_diff",
    )(*([proj] * (3 * nh)), tab64, tab64, qg, kg, lam_p, sg)


def _attn_b_kernel(*refs, tq, nblk):
    nh = B_HEADS
    q_refs, k_refs, v_refs = refs[0:nh], refs[nh:2 * nh], refs[2 * nh:3 * nh]
    tabq_ref, tabk_ref, qg_ref, kg_ref, o_ref, ksa_ref, va_ref, kmean_ref, s_ref = refs[3 * nh:]
    qt = pl.program_id(1)
    s_len = k_refs[0].shape[0]

    @pl.when(qt == 0)
    def _():
        blk = lax.shift_right_logical(lax.broadcasted_iota(I32, (s_len, LANES), 0), MOBA_BLOCK.bit_length() - 1)
        onehot = jnp.where(lax.broadcasted_iota(I32, (s_len, LANES), 1) == blk, 1.0, 0.0).astype(BF16)
        ones = _ones_col(s_len)
        for h in range(nh):
            k = _rope(_rms_full(k_refs[h][...].astype(F32), kg_ref[...]), tabk_ref, 16)
            ksa_ref[h, :, :LANES] = k.astype(BF16)
            ksa_ref[h, :, LANES:] = onehot
            kmean_ref[h] = jnp.zeros((LANES, LANES), F32)
            kmean_ref[h, 0:nblk, :] = jnp.mean(k.reshape(nblk, MOBA_BLOCK, HEAD_DIM), axis=1)
            va_ref[h, :, :LANES] = v_refs[h][...]
            va_ref[h, :, LANES:] = ones

    lane = lax.broadcasted_iota(I32, (tq, LANES), 1)
    q_aug = []
    for h in range(nh):
        q = _rope(_rms_full(q_refs[h][...].astype(F32), qg_ref[...]), tabq_ref, 16)
        gate = lax.dot_general(q, kmean_ref[h], _NT, preferred_element_type=F32,
                               precision=lax.Precision.HIGHEST)
        g = jnp.where(lane < qt, gate, -jnp.inf)
        keep = lane == qt
        for _ in range(MOBA_TOPK):
            mx = jnp.max(g, axis=-1, keepdims=True)
            first = jnp.min(jnp.where(g == mx, lane, LANES), axis=-1, keepdims=True)
            pick = jnp.logical_and(lane == first, mx > -jnp.inf)
            keep = jnp.logical_or(keep, pick)
            g = jnp.where(pick, -jnp.inf, g)
        bias = jnp.where(keep, 0.0, NEG)
        q_aug.append(jnp.concatenate([q * (HEAD_DIM ** -0.5 * LOG2E), bias], axis=1).astype(BF16))

    causal = lax.broadcasted_iota(I32, (tq, tq), 1) <= lax.broadcasted_iota(I32, (tq, tq), 0)
    outs = _attend(q_aug, [ksa_ref.at[h] for h in range(nh)], [va_ref.at[h] for h in range(nh)],
                   [s_ref.at[h] for h in range(nh)], qt, tq, diag_mask=causal)
    for h in range(nh):
        o_ref[:, h * HEAD_DIM:(h + 1) * HEAD_DIM] = outs[h].astype(o_ref.dtype)


def _attn_b(proj, tab128, qg, kg, *, b, s):
    tq = MOBA_BLOCK
    nq = s // tq
    nh = B_HEADS
    kern = functools.partial(_attn_b_kernel, tq=tq, nblk=nq)
    small = lambda shape: pl.BlockSpec(shape, lambda bi, qt: (0,) * len(shape))
    qspec = lambda h: pl.BlockSpec((tq, LANES), lambda bi, qt: (bi * nq + qt, QB_C + h))
    kspec = lambda c0, h: pl.BlockSpec((s, LANES), lambda bi, qt: (bi, c0 + h))
    return pl.pallas_call(
        kern,
        grid=(b, nq),
        in_specs=([qspec(h) for h in range(nh)] + [kspec(KB_C, h) for h in range(nh)]
                  + [kspec(VB_C, h) for h in range(nh)]
                  + [pl.BlockSpec((3, tq, LANES), lambda bi, qt: (0, qt, 0)),
                     pl.BlockSpec((3, s, LANES), lambda bi, qt: (0, 0, 0)),
                     small((1, LANES)), small((1, LANES))]),
        out_specs=pl.BlockSpec((tq, nh * HEAD_DIM), lambda bi, qt: (bi * nq + qt, 0)),
        out_shape=jax.ShapeDtypeStruct((b * s, nh * HEAD_DIM), BF16),
        scratch_shapes=[pltpu.VMEM((nh, s, 2 * LANES), BF16), pltpu.VMEM((nh, s, 2 * LANES), BF16),
                        pltpu.VMEM((nh, LANES, LANES), F32), pltpu.VMEM((nh, nq, tq, tq), F32)],
        compiler_params=_cp(("parallel", "arbitrary")),
        name="attn_moba",
    )(*([proj] * (3 * nh)), tab128, tab128, qg, kg)


def _attn_c_kernel(q0_ref, q1_ref, q2_ref, q3_ref, q4_ref, qi_ref, kc_ref, vc_ref, kt_ref, wt_ref,
                   t128q_ref, t128k_ref, t64q_ref, t64k_ref, cqg_ref, ckg_ref, ikg_ref,
                   o_ref, kcs_ref, kis_ref, va_ref, keys_ref, s_ref, *, tq, tk, nq, n_sel):
    qt = pl.program_id(1)

    @pl.when(qt == 0)
    def _():
        k = _rope(_rms_full(kc_ref[...].astype(F32), ckg_ref[...]), t128k_ref, 16)
        kcs_ref[...] = k.astype(BF16)
        va_ref[:, :LANES] = vc_ref[...]
        va_ref[:, LANES:] = _ones_col(vc_ref.shape[0])
        t = kt_ref[...]
        lane = lax.broadcasted_iota(I32, t.shape, 1)
        ms = jnp.sum(jnp.where(lane < IDX_DIM, t * t, 0.0), axis=-1, keepdims=True) * (1.0 / IDX_DIM)
        ki = _rope(t * lax.rsqrt(ms + EPS) * ikg_ref[...], t64k_ref, 8)
        kis_ref[...] = (ki + pltpu.roll(ki, 64, 1)).astype(BF16)

    lane = lax.broadcasted_iota(I32, (tq, LANES), 1)
    heads = []
    for j in range(IDX_HEADS // 2):
        x = _rope(qi_ref[:, j * LANES:(j + 1) * LANES].astype(F32), t64q_ref, 8)
        heads.append(jnp.where(lane < 64, x, 0.0).astype(BF16))
        heads.append(jnp.where(lane < 64, 0.0, x).astype(BF16))
    hg = 4
    qi_groups = [jnp.concatenate(heads[g:g + hg], axis=0) for g in range(0, IDX_HEADS, hg)]
    wt = wt_ref[...] * ((IDX_HEADS ** -0.5) * (IDX_DIM ** -0.5))
    w_cols = [wt[:, IDX_DIM + h:IDX_DIM + h + 1] for h in range(IDX_HEADS)]

    row = qt * tq + lax.broadcasted_iota(I32, (tq, tk), 0)
    col0 = lax.broadcasted_iota(I32, (tq, tk), 1)
    nch = qt + 1

    def index_keys(c, _):
        off = pl.multiple_of(c * tk, tk)
        kchunk = kis_ref[pl.ds(off, tk), :]
        sc = jnp.zeros((tq, tk), F32)
        for g, qg in enumerate(qi_groups):
            lg = lax.dot_general(qg, kchunk, _NT, preferred_element_type=F32)
            for i in range(hg):
                sc = sc + jnp.maximum(lg[i * tq:(i + 1) * tq], 0.0) * w_cols[g * hg + i]
        bits = pltpu.bitcast(sc, I32)
        key = jnp.where(bits < 0, bits ^ jnp.int32(0x7FFFFFFF), bits)
        keys_ref[c] = jnp.where(col0 + c * tk <= row, key, jnp.int32(INT_MIN))
        return 0

    lax.fori_loop(0, nch, index_keys, 0)

    def count_ge(cand):
        def body(c, acc):
            a = jnp.where(keys_ref[c] >= cand, 1.0, 0.0)
            return acc + (a[:, :LANES] + a[:, LANES:])
        tot = lax.fori_loop(0, nch, body, jnp.zeros((tq, LANES), F32))
        return jnp.sum(tot, axis=-1, keepdims=True)

    lo = jnp.where(count_ge(jnp.zeros((tq, 1), I32)) >= n_sel, jnp.int32(0), jnp.int32(INT_MIN))

    def bit_step(i, lo):
        cand = lo + lax.shift_left(jnp.int32(1), 30 - i)
        return jnp.where(count_ge(cand) >= n_sel, cand, lo)

    lo = lax.fori_loop(0, 31, bit_step, lo)
    thr = jnp.maximum(lo, jnp.int32(INT_MIN + 1))

    qs = []
    for q_ref in (q0_ref, q1_ref, q2_ref, q3_ref, q4_ref):
        q = _rope(_rms_full(q_ref[...].astype(F32), cqg_ref[...]), t128q_ref, 16)
        qs.append((q * (HEAD_DIM ** -0.5 * LOG2E)).astype(BF16))
    q_all = jnp.concatenate(qs, axis=0)
    rows = C_HEADS * tq

    def select(c, s):
        return jnp.where((keys_ref[c] >= thr)[None], s.reshape(C_HEADS, tq, tk), NEG).reshape(rows, tk)

    o, = _attend([q_all], [kcs_ref], [va_ref], [s_ref], qt, tk, mask_fn=select)
    for h in range(C_HEADS):
        o_ref[:, h * HEAD_DIM:(h + 1) * HEAD_DIM] = o[h * tq:(h + 1) * tq].astype(o_ref.dtype)


def _attn_c(proj, tail, tab128, tab64, cqg, ckg, ikg, *, b, s):
    tq, tk = 256, 256
    nq = s // tq
    n_sel = min(DSA_TOPK, s // 4)
    assert tq == tk
    kern = functools.partial(_attn_c_kernel, tq=tq, tk=tk, nq=nq, n_sel=n_sel)
    small = lambda shape: pl.BlockSpec(shape, lambda bi, qt: (0,) * len(shape))
    qspec = lambda h: pl.BlockSpec((tq, LANES), lambda bi, qt: (bi * nq + qt, QC_C + h))
    return pl.pallas_call(
        kern,
        grid=(b, nq),
        in_specs=[
            qspec(0), qspec(1), qspec(2), qspec(3), qspec(4),
            pl.BlockSpec((tq, IDX_HEADS * IDX_DIM), lambda bi, qt: (bi * nq + qt, QI_C // 8)),
            pl.BlockSpec((s, LANES), lambda bi, qt: (bi, KC_C)),
            pl.BlockSpec((s, LANES), lambda bi, qt: (bi, VC_C)),
            pl.BlockSpec((s, LANES), lambda bi, qt: (bi, 0)),
            pl.BlockSpec((tq, LANES), lambda bi, qt: (bi * nq + qt, 0)),
            pl.BlockSpec((3, tq, LANES), lambda bi, qt: (0, qt, 0)),
            pl.BlockSpec((3, s, LANES), lambda bi, qt: (0, 0, 0)),
            pl.BlockSpec((3, tq, LANES), lambda bi, qt: (0, qt, 0)),
            pl.BlockSpec((3, s, LANES), lambda bi, qt: (0, 0, 0)),
            small((1, LANES)), small((1, LANES)), small((1, LANES)),
        ],
        out_specs=pl.BlockSpec((tq, C_HEADS * HEAD_DIM), lambda bi, qt: (bi * nq + qt, 0)),
        out_shape=jax.ShapeDtypeStruct((b * s, C_HEADS * HEAD_DIM), BF16),
        scratch_shapes=[pltpu.VMEM((s, LANES), BF16), pltpu.VMEM((s, LANES), BF16),
                        pltpu.VMEM((s, 2 * LANES), BF16), pltpu.VMEM((s // tk, tq, tk), I32),
                        pltpu.VMEM((s // tk, C_HEADS * tq, tk), F32)],
        compiler_params=_cp(("parallel", "arbitrary")),
        name="attn_dsa",
    )(proj, proj, proj, proj, proj, proj, proj, proj, tail, tail,
      tab128, tab128, tab64, tab64, cqg, ckg, ikg)


def _out_router_kernel(x_ref, oa_ref, ob_ref, oc_ref, w_ref, g2_ref, wr_ref, br_ref,
                       x1_ref, t_ref, eid_ref, gate_ref):
    na = A_HEADS * HEAD_DIM
    nb = na + B_HEADS * HEAD_DIM
    x1 = (x_ref[...]
          + jnp.dot(oa_ref[...], w_ref[0:na, :], preferred_element_type=F32)
          + jnp.dot(ob_ref[...], w_ref[na:nb, :], preferred_element_type=F32)
          + jnp.dot(oc_ref[...], w_ref[nb:, :], preferred_element_type=F32))
    x1_ref[...] = x1
    t = _rms_full(x1, g2_ref[...])
    t_ref[...] = t

    lg = jnp.dot(t, wr_ref[...], preferred_element_type=F32, precision=lax.Precision.HIGHEST) + br_ref[...]
    lane = lax.broadcasted_iota(I32, lg.shape, 1)
    ninf = -jnp.inf
    gl = jnp.where(lane < N_GROUPS, lg, ninf)
    gm = jnp.max(gl, axis=-1, keepdims=True)
    ge = jnp.exp(gl - gm)
    g_prob = ge / jnp.sum(ge, axis=-1, keepdims=True)
    g_idx = jnp.min(jnp.where(gl == gm, lane, LANES), axis=-1, keepdims=True)
    g_w = jnp.sum(jnp.where(lane == g_idx, g_prob, 0.0), axis=-1, keepdims=True)

    e0 = N_GROUPS + g_idx * EXPERTS_PER_GROUP
    emask = jnp.logical_and(lane >= e0, lane < e0 + EXPERTS_PER_GROUP)
    el = jnp.where(emask, lg, ninf)
    em = jnp.max(el, axis=-1, keepdims=True)
    ee = jnp.exp(el - em)
    ep = jnp.where(emask, ee / jnp.sum(ee, axis=-1, keepdims=True), ninf)
    v1 = jnp.max(ep, axis=-1, keepdims=True)
    i1 = jnp.min(jnp.where(ep == v1, lane, LANES), axis=-1, keepdims=True)
    ep2 = jnp.where(lane == i1, ninf, ep)
    v2 = jnp.max(ep2, axis=-1, keepdims=True)
    i2 = jnp.min(jnp.where(ep2 == v2, lane, LANES), axis=-1, keepdims=True)
    den = v1 + v2
    eid_ref[...] = jnp.where(lane == 0, i1 - N_GROUPS, jnp.where(lane == 1, i2 - N_GROUPS, 0))
    gate_ref[...] = jnp.where(lane == 0, g_w * (v1 / den), jnp.where(lane == 1, g_w * (v2 / den), 0.0))


def _out_router(x, oa, ob, oc, w_out_bf, g2, w_router, b_router):
    t, d = x.shape
    tm = 256
    row = lambda c: pl.BlockSpec((tm, c), lambda i: (i, 0))
    full = lambda r, c: pl.BlockSpec((r, c), lambda i: (0, 0))
    return pl.pallas_call(
        _out_router_kernel,
        grid=(t // tm,),
        in_specs=[row(d), row(oa.shape[1]), row(ob.shape[1]), row(oc.shape[1]),
                  full(d, d), full(1, d), full(d, LANES), full(1, LANES)],
        out_specs=[row(d), row(d), row(LANES), row(LANES)],
        out_shape=[jax.ShapeDtypeStruct((t, d), F32), jax.ShapeDtypeStruct((t, d), F32),
                   jax.ShapeDtypeStruct((t, LANES), I32), jax.ShapeDtypeStruct((t, LANES), F32)],
        compiler_params=_cp(("parallel",)),
        name="out_proj_router",
    )(x, oa, ob, oc, w_out_bf, g2.reshape(1, d), w_router, b_router)


def _moe_kernel(texp_ref, nused_ref, rtok_ref, t_hbm, wg_ref, wu_ref, wd_ref,
                y_ref, xbuf, sem, wg_bf, wu_bf, wd_bf, *, tm):
    i = pl.program_id(0)
    nu = nused_ref[0]

    def gather(tile, slot):
        def body(r, _):
            tok = rtok_ref[tile * tm + r]
            pltpu.make_async_copy(t_hbm.at[pl.ds(tok, 1)], xbuf.at[slot, pl.ds(r, 1)], sem.at[slot]).start()
            return 0
        lax.fori_loop(0, tm, body, 0)

    @pl.when(i == 0)
    def _():
        gather(0, 0)

    @pl.when(i + 1 < nu)
    def _():
        gather(i + 1, (i + 1) % 2)

    @pl.when(i < nu)
    def _():
        slot = i % 2
        pltpu.make_async_copy(xbuf.at[slot], xbuf.at[slot], sem.at[slot]).wait()

        changed = jnp.logical_or(i == 0, texp_ref[i] != texp_ref[jnp.maximum(i - 1, 0)])

        @pl.when(changed)
        def _():
            wg_bf[...] = wg_ref[...].astype(BF16)
            wu_bf[...] = wu_ref[...].astype(BF16)
            wd_bf[...] = wd_ref[...].astype(BF16)

        x = xbuf[slot].astype(BF16)
        g = jnp.dot(x, wg_bf[...], preferred_element_type=F32)
        u = jnp.dot(x, wu_bf[...], preferred_element_type=F32)
        h = (g * jax.nn.sigmoid(g)) * u
        y_ref[...] = jnp.dot(h.astype(BF16), wd_bf[...], preferred_element_type=F32)

    @pl.when(i >= nu)
    def _():
        y_ref[...] = jnp.zeros_like(y_ref)


def _moe(t, tile_expert, n_used, row_token, w_gate, w_up, w_down, *, layer, n_tiles):
    tm = MOE_TILE
    d = t.shape[1]
    grid_spec = pltpu.PrefetchScalarGridSpec(
        num_scalar_prefetch=3,
        grid=(n_tiles,),
        in_specs=[
            pl.BlockSpec(memory_space=pl.ANY),
            pl.BlockSpec((None, None, d, EXPERT_FF), lambda i, te, nu, rt: (layer, te[i], 0, 0)),
            pl.BlockSpec((None, None, d, EXPERT_FF), lambda i, te, nu, rt: (layer, te[i], 0, 0)),
            pl.BlockSpec((None, None, EXPERT_FF, d), lambda i, te, nu, rt: (layer, te[i], 0, 0)),
        ],
        out_specs=pl.BlockSpec((tm, d), lambda i, te, nu, rt: (i, 0)),
        scratch_shapes=[pltpu.VMEM((2, tm, d), F32), pltpu.SemaphoreType.DMA((2,)),
                        pltpu.VMEM((d, EXPERT_FF), BF16), pltpu.VMEM((d, EXPERT_FF), BF16),
                        pltpu.VMEM((EXPERT_FF, d), BF16)],
    )
    return pl.pallas_call(
        functools.partial(_moe_kernel, tm=tm),
        grid_spec=grid_spec,
        out_shape=jax.ShapeDtypeStruct((n_tiles * tm, d), F32),
        compiler_params=_cp(("arbitrary",)),
        name="moe_experts",
    )(tile_expert, n_used, row_token, t, w_gate, w_up, w_down)


def _combine_kernel(pos_ref, x1_ref, gate_ref, y_hbm, o_ref, buf, sem, *, tc):
    i = pl.program_id(0)
    n = pl.num_programs(0)

    def gather(tile, slot):
        def body(r, _):
            a = (tile * tc + r) * 2
            pltpu.make_async_copy(y_hbm.at[pl.ds(pos_ref[a], 1)], buf.at[slot, 0, pl.ds(r, 1)], sem.at[slot]).start()
            pltpu.make_async_copy(y_hbm.at[pl.ds(pos_ref[a + 1], 1)], buf.at[slot, 1, pl.ds(r, 1)], sem.at[slot]).start()
            return 0
        lax.fori_loop(0, tc, body, 0)

    @pl.when(i == 0)
    def _():
        gather(0, 0)

    @pl.when(i + 1 < n)
    def _():
        gather(i + 1, (i + 1) % 2)

    slot = i % 2
    pltpu.make_async_copy(buf.at[slot], buf.at[slot], sem.at[slot]).wait()
    gt = gate_ref[...]
    o_ref[...] = x1_ref[...] + gt[:, 0:1] * buf[slot, 0] + gt[:, 1:2] * buf[slot, 1]


def _combine(pos, x1, gate, y):
    t, d = x1.shape
    tc = 256
    grid_spec = pltpu.PrefetchScalarGridSpec(
        num_scalar_prefetch=1,
        grid=(t // tc,),
        in_specs=[pl.BlockSpec((tc, d), lambda i, p: (i, 0)), pl.BlockSpec((tc, LANES), lambda i, p: (i, 0)),
                  pl.BlockSpec(memory_space=pl.ANY)],
        out_specs=pl.BlockSpec((tc, d), lambda i, p: (i, 0)),
        scratch_shapes=[pltpu.VMEM((2, 2, tc, d), F32), pltpu.SemaphoreType.DMA((2,))],
    )
    return pl.pallas_call(
        functools.partial(_combine_kernel, tc=tc),
        grid_spec=grid_spec,
        out_shape=jax.ShapeDtypeStruct((t, d), F32),
        compiler_params=_cp(("arbitrary",)),
        name="moe_combine",
    )(pos, x1, gate, y)


def _rope_lane_tables(seq, dim):
    rot = dim // ROPE_FRAC
    half = rot // 2
    inv = 1.0 / (ROPE_THETA ** (jnp.arange(0, rot, 2, dtype=F32) / rot))
    ang = jnp.arange(seq, dtype=F32)[:, None] * inv[None, :]
    cos, sin = jnp.cos(ang), jnp.sin(ang)
    z_half = jnp.zeros((seq, half), F32)
    z_rest = jnp.zeros((seq, dim - rot), F32)
    c = jnp.concatenate([cos, cos, jnp.ones((seq, dim - rot), F32)], axis=-1)
    sa = jnp.concatenate([-sin, z_half, z_rest], axis=-1)
    sb = jnp.concatenate([z_half, sin, z_rest], axis=-1)
    reps = LANES // dim
    return jnp.stack([jnp.tile(c, (1, reps)), jnp.tile(sa, (1, reps)), jnp.tile(sb, (1, reps))])


def _routing_plan(eid, n_tiles):
    tm = MOE_TILE
    e = eid[:, :2].reshape(-1)
    n_assign = e.shape[0]
    onehot = (e[:, None] == jnp.arange(N_EXPERTS, dtype=I32)[None, :]).astype(I32)
    csum = jnp.cumsum(onehot, axis=0)
    rank = jnp.take_along_axis(csum, e[:, None], axis=1)[:, 0] - 1
    counts = csum[-1]
    tiles_per = (counts + tm - 1) // tm
    tile_end = jnp.cumsum(tiles_per)
    tile_start = tile_end - tiles_per
    n_used = tile_end[-1]
    pos = tile_start[e] * tm + rank
    tile_ids = jnp.minimum(jnp.arange(n_tiles, dtype=I32), n_used - 1)
    tile_expert = jnp.sum((tile_end[None, :] <= tile_ids[:, None]).astype(I32), axis=1)
    row_token = jnp.zeros((n_tiles * tm,), I32).at[pos].set(jnp.arange(n_assign, dtype=I32) // 2)
    return tile_expert, n_used.reshape(1).astype(I32), row_token, pos.astype(I32)


def kernel(x, norm1_g, w_in, a_qn_g, a_kn_g, a_lambda, a_subln_g, b_qn_g, b_kn_g, c_qn_g, c_kn_g,
           idx_kn_g, w_out, norm2_g, w_group, b_group, w_expert, b_expert, w_gate, w_up, w_down):
    b, s, d = x.shape
    depth = w_in.shape[0]
    assert d == D_MODEL and s % MOBA_BLOCK == 0
    t = b * s
    n_tiles = (2 * t) // MOE_TILE + N_EXPERTS
    tab64 = _rope_lane_tables(s, A_QK)
    tab128 = _rope_lane_tables(s, HEAD_DIM)
    tile2 = lambda v: jnp.tile(v, 2).reshape(1, LANES)
    row = lambda v: v.reshape(1, LANES)

    xf = x.reshape(t, d)
    for l in range(depth):
        lam_init = 0.8 - 0.6 * math.exp(-0.3 * l)
        h = _rmsnorm_bf16(xf, norm1_g[l])
        w_main = _cast_bf16(w_in, l, MAIN_COLS)
        w_tail = jnp.pad(w_in[l, :, MAIN_COLS:], ((0, 0), (0, LANES - TAIL_COLS)))
        proj = _matmul(h, w_main, BF16, 1024, 1024, "in_proj")
        tail = _matmul(h, w_tail, F32, 1024, LANES, "in_proj_tail")

        oa = _attn_a(proj, tab64, tile2(a_qn_g[l]), tile2(a_kn_g[l]), a_lambda[l], row(a_subln_g[l]),
                     b=b, s=s, lam_init=lam_init)
        ob = _attn_b(proj, tab128, row(b_qn_g[l]), row(b_kn_g[l]), b=b, s=s)
        ikg = jnp.pad(idx_kn_g[l], (0, LANES - IDX_DIM)).reshape(1, LANES)
        oc = _attn_c(proj, tail, tab128, tab64, row(c_qn_g[l]), row(c_kn_g[l]), ikg, b=b, s=s)

        w_router = jnp.pad(jnp.concatenate([w_group[l], w_expert[l]], axis=1),
                           ((0, 0), (0, LANES - N_GROUPS - N_EXPERTS)))
        b_router = jnp.pad(jnp.concatenate([b_group[l], b_expert[l]]),
                           (0, LANES - N_GROUPS - N_EXPERTS)).reshape(1, LANES)
        x1, tn, eid, gate = _out_router(xf, oa, ob, oc, _cast_bf16(w_out, l, d), norm2_g[l], w_router, b_router)

        tile_expert, n_used, row_token, pos = _routing_plan(eid, n_tiles)
        y = _moe(tn, tile_expert, n_used, row_token, w_gate, w_up, w_down, layer=l, n_tiles=n_tiles)
        xf = _combine(pos, x1, gate, y)
    return xf.reshape(b, s, d)
```

```python
import functools
import math

import jax
import jax.numpy as jnp
from jax import lax
from jax.experimental import pallas as pl
from jax.experimental.pallas import tpu as pltpu

F32 = jnp.float32
BF16 = jnp.bfloat16
I32 = jnp.int32

D_MODEL = 2048
HEAD_DIM = 128
A_HEADS = 6
A_QK = 64
B_HEADS = 5
MOBA_BLOCK = 256
MOBA_TOPK = 3
C_HEADS = 5
IDX_HEADS = 16
IDX_DIM = 64
DSA_TOPK = 256
ROPE_THETA = 500000.0
ROPE_FRAC = 4
EPS = 1e-6
N_GROUPS = 4
EXPERTS_PER_GROUP = 8
N_EXPERTS = N_GROUPS * EXPERTS_PER_GROUP
EXPERT_FF = 512

LANES = 128
MAIN_COLS = 6144
TAIL_COLS = 80
QA_C, KA_C, VA_C = 0, 6, 12
QB_C, KB_C, VB_C = 18, 23, 28
QC_C, KC_C, VC_C = 33, 38, 39
QI_C = 40
NEG = -1e30
LOG2E = 1.4426950408889634
INT_MIN = -(2 ** 31)
MOE_TILE = 256
VMEM_LIMIT = 56 * 1024 * 1024

_NT = (((1,), (1,)), ((), ()))


def _cp(sem, vmem=VMEM_LIMIT):
    return pltpu.CompilerParams(dimension_semantics=sem, vmem_limit_bytes=vmem)


def _rms_full(x, g):
    ms = jnp.mean(x * x, axis=-1, keepdims=True)
    return x * lax.rsqrt(ms + EPS) * g


def _rms_halves(x, g):
    lane = lax.broadcasted_iota(I32, x.shape, 1)
    lo = lane < 64
    x2 = x * x
    s_lo = jnp.sum(jnp.where(lo, x2, 0.0), axis=-1, keepdims=True)
    s_hi = jnp.sum(jnp.where(lo, 0.0, x2), axis=-1, keepdims=True)
    r = jnp.where(lo, lax.rsqrt(s_lo * (1.0 / 64) + EPS), lax.rsqrt(s_hi * (1.0 / 64) + EPS))
    return x * r * g


def _rope(x, tab_ref, half):
    c = tab_ref[0]
    sa = tab_ref[1]
    sb = tab_ref[2]
    return x * c + pltpu.roll(x, LANES - half, 1) * sa + pltpu.roll(x, half, 1) * sb


def _ones_col(rows):
    lane = lax.broadcasted_iota(I32, (rows, LANES), 1)
    return jnp.where(lane == 0, 1.0, 0.0).astype(BF16)


def _lane_max2(mrun, s):
    return jnp.maximum(mrun, jnp.maximum(s[:, :LANES], s[:, LANES:]))


def _attend(qs, k_refs, va_refs, s_refs, qt, tk, *, diag_mask=None, mask_fn=None):
    n = len(qs)
    rows = qs[0].shape[0]
    static = isinstance(qt, int)

    def loop(lo, hi, body, init):
        if not static:
            return lax.fori_loop(lo, hi, body, init)
        for c in range(lo, hi):
            init = body(c, init)
        return init

    def chunk(ref, c):
        if static:
            return ref[c * tk:(c + 1) * tk, :]
        return ref[pl.ds(pl.multiple_of(c * tk, tk), tk), :]

    def scores(h, c):
        s = lax.dot_general(qs[h], chunk(k_refs[h], c), _NT, preferred_element_type=F32)
        return s if mask_fn is None else mask_fn(c, s)

    def first(c, mruns):
        out = []
        for h in range(n):
            s = scores(h, c)
            s_refs[h][c] = s
            out.append(_lane_max2(mruns[h], s))
        return tuple(out)

    init = tuple(jnp.full((rows, LANES), NEG, F32) for _ in range(n))
    if diag_mask is None:
        mruns = loop(0, qt + 1, first, init)
    else:
        mruns = list(loop(0, qt, first, init))
        for h in range(n):
            s = jnp.where(diag_mask, scores(h, qt), NEG)
            s_refs[h][qt] = s
            mruns[h] = _lane_max2(mruns[h], s)
    ms = [jnp.max(mr, axis=-1, keepdims=True) for mr in mruns]

    def second(c, accs):
        return tuple(
            accs[h] + jnp.dot(jnp.exp2(s_refs[h][c] - ms[h]).astype(BF16), chunk(va_refs[h], c),
                              preferred_element_type=F32)
            for h in range(n))

    accs = loop(0, qt + 1, second, tuple(jnp.zeros((rows, 2 * LANES), F32) for _ in range(n)))
    return [a[:, :LANES] / a[:, LANES:LANES + 1] for a in accs]


def _attend_unrolled(nq, qt, finish, qs, k_refs, va_refs, s_refs, tk, **kwargs):
    for j in range(nq):
        pl.when(qt == j)(lambda j=j: finish(_attend(qs, k_refs, va_refs, s_refs, j, tk, **kwargs)))


def _norm_kernel(x_ref, g_ref, o_ref):
    o_ref[...] = _rms_full(x_ref[...], g_ref[...]).astype(o_ref.dtype)


def _rmsnorm_bf16(x, g):
    t, d = x.shape
    tm = 512
    return pl.pallas_call(
        _norm_kernel,
        grid=(t // tm,),
        in_specs=[pl.BlockSpec((tm, d), lambda i: (i, 0)), pl.BlockSpec((1, d), lambda i: (0, 0))],
        out_specs=pl.BlockSpec((tm, d), lambda i: (i, 0)),
        out_shape=jax.ShapeDtypeStruct((t, d), BF16),
        compiler_params=_cp(("parallel",)),
        name="rmsnorm",
    )(x, g.reshape(1, d))


def _cast_kernel(x_ref, o_ref):
    o_ref[...] = x_ref[...].astype(o_ref.dtype)


def _cast_bf16(w, layer, ncols):
    r = w.shape[1]
    tr, tc = 256, min(ncols, 1024)
    return pl.pallas_call(
        _cast_kernel,
        grid=(r // tr, ncols // tc),
        in_specs=[pl.BlockSpec((None, tr, tc), lambda i, j: (layer, i, j))],
        out_specs=pl.BlockSpec((tr, tc), lambda i, j: (i, j)),
        out_shape=jax.ShapeDtypeStruct((r, ncols), BF16),
        compiler_params=_cp(("parallel", "parallel")),
        name="cast_bf16",
    )(w)


def _matmul_kernel(x_ref, w_ref, o_ref):
    w = w_ref[...].astype(x_ref.dtype)
    o_ref[...] = jnp.dot(x_ref[...], w, preferred_element_type=F32).astype(o_ref.dtype)


def _matmul(x, w, layer, out_dtype, tm, tn, name):
    m, k = x.shape
    n = w.shape[2]
    return pl.pallas_call(
        _matmul_kernel,
        grid=(n // tn, m // tm),
        in_specs=[pl.BlockSpec((tm, k), lambda j, i: (i, 0)),
                  pl.BlockSpec((None, k, tn), lambda j, i: (layer, 0, j))],
        out_specs=pl.BlockSpec((tm, tn), lambda j, i: (i, j)),
        out_shape=jax.ShapeDtypeStruct((m, n), out_dtype),
        compiler_params=_cp(("parallel", "parallel")),
        name=name,
    )(x, w)


def _attn_a_kernel(*refs, tq, nq, nh, lam_init):
    q_refs, k_refs, v_refs = refs[0:nh], refs[nh:2 * nh], refs[2 * nh:3 * nh]
    tabq_ref, tabk_ref, qg_ref, kg_ref, lam_ref, sg_ref, o_ref, ks_ref, va_ref, s_ref = refs[3 * nh:]
    qt = pl.program_id(2)

    @pl.when(qt == 0)
    def _():
        ones = _ones_col(ks_ref.shape[1])
        for h in range(nh):
            k = _rms_halves(k_refs[h][...].astype(F32), kg_ref[...])
            ks_ref[h] = _rope(k, tabk_ref, 8).astype(BF16)
            va_ref[h, :, :LANES] = v_refs[h][...]
            va_ref[h, :, LANES:] = ones

    lane = lax.broadcasted_iota(I32, (tq, LANES), 1)
    qs = []
    for h in range(nh):
        q = _rms_halves(q_refs[h][...].astype(F32), qg_ref[...])
        q = _rope(q, tabq_ref, 8) * (A_QK ** -0.5 * LOG2E)
        qs.append(jnp.concatenate([jnp.where(lane < 64, q, 0.0), jnp.where(lane < 64, 0.0, q)],
                                  axis=0).astype(BF16))

    lp = lam_ref[...]
    lam = (jnp.exp(jnp.sum(lp[0:1] * lp[1:2], axis=-1, keepdims=True))
           - jnp.exp(jnp.sum(lp[2:3] * lp[3:4], axis=-1, keepdims=True)) + lam_init)

    row = lax.broadcasted_iota(I32, (2 * tq, tq), 0)
    row = jnp.where(row >= tq, row - tq, row)
    causal = lax.broadcasted_iota(I32, (2 * tq, tq), 1) <= row
    def finish(outs):
        for h in range(nh):
            o = outs[h][:tq] - lam * outs[h][tq:]
            o = _rms_full(o, sg_ref[...]) * (1.0 - lam_init)
            o_ref[:, h * HEAD_DIM:(h + 1) * HEAD_DIM] = o.astype(o_ref.dtype)

    _attend_unrolled(nq, qt, finish, qs, [ks_ref.at[h] for h in range(nh)], [va_ref.at[h] for h in range(nh)],
                     [s_ref.at[h] for h in range(nh)], tq, diag_mask=causal)


def _attn_a(proj, tab64, qg, kg, lam_p, sg, *, b, s, lam_init):
    tq = 256
    nq = s // tq
    nh = 1
    ng = A_HEADS // nh
    kern = functools.partial(_attn_a_kernel, tq=tq, nq=nq, nh=nh, lam_init=lam_init)
    small = lambda shape: pl.BlockSpec(shape, lambda bi, g, qt: (0,) * len(shape))
    qspec = lambda h: pl.BlockSpec((tq, LANES), lambda bi, g, qt: (bi * nq + qt, QA_C + g * nh + h))
    kspec = lambda c0, h: pl.BlockSpec((s, LANES), lambda bi, g, qt: (bi, c0 + g * nh + h))
    return pl.pallas_call(
        kern,
        grid=(b, ng, nq),
        in_specs=([qspec(h) for h in range(nh)] + [kspec(KA_C, h) for h in range(nh)]
                  + [kspec(VA_C, h) for h in range(nh)]
                  + [pl.BlockSpec((3, tq, LANES), lambda bi, g, qt: (0, qt, 0)),
                     pl.BlockSpec((3, s, LANES), lambda bi, g, qt: (0, 0, 0)),
                     small((1, LANES)), small((1, LANES)), small((4, A_QK)), small((1, LANES))]),
        out_specs=pl.BlockSpec((tq, nh * HEAD_DIM), lambda bi, g, qt: (bi * nq + qt, g)),
        out_shape=jax.ShapeDtypeStruct((b * s, A_HEADS * HEAD_DIM), BF16),
        scratch_shapes=[pltpu.VMEM((nh, s, LANES), BF16), pltpu.VMEM((nh, s, 2 * LANES), BF16),
                        pltpu.VMEM((nh, nq, 2 * tq, tq), F32)],
        compiler_params=_cp(("parallel", "parallel", "arbitrary")),
        name="attn_diff",
    )(*([proj] * (3 * nh)), tab64, tab64, qg, kg, lam_p, sg)


def _attn_b_kernel(*refs, tq, nblk):
    nh = B_HEADS
    q_refs, k_refs, v_refs = refs[0:nh], refs[nh:2 * nh], refs[2 * nh:3 * nh]
    tabq_ref, tabk_ref, qg_ref, kg_ref, o_ref, ksa_ref, va_ref, kmean_ref, s_ref = refs[3 * nh:]
    qt = pl.program_id(1)
    s_len = k_refs[0].shape[0]

    @pl.when(qt == 0)
    def _():
        blk = lax.shift_right_logical(lax.broadcasted_iota(I32, (s_len, LANES), 0), MOBA_BLOCK.bit_length() - 1)
        onehot = jnp.where(lax.broadcasted_iota(I32, (s_len, LANES), 1) == blk, 1.0, 0.0).astype(BF16)
        ones = _ones_col(s_len)
        for h in range(nh):
            k = _rope(_rms_full(k_refs[h][...].astype(F32), kg_ref[...]), tabk_ref, 16)
            ksa_ref[h, :, :LANES] = k.astype(BF16)
            ksa_ref[h, :, LANES:] = onehot
            kmean_ref[h] = jnp.zeros((LANES, LANES), F32)
            kmean_ref[h, 0:nblk, :] = jnp.mean(k.reshape(nblk, MOBA_BLOCK, HEAD_DIM), axis=1)
            va_ref[h, :, :LANES] = v_refs[h][...]
            va_ref[h, :, LANES:] = ones

    lane = lax.broadcasted_iota(I32, (tq, LANES), 1)
    q_aug = []
    for h in range(nh):
        q = _rope(_rms_full(q_refs[h][...].astype(F32), qg_ref[...]), tabq_ref, 16)
        gate = lax.dot_general(q, kmean_ref[h], _NT, preferred_element_type=F32,
                               precision=lax.Precision.HIGHEST)
        g = jnp.where(lane < qt, gate, -jnp.inf)
        keep = lane == qt
        for _ in range(MOBA_TOPK):
            mx = jnp.max(g, axis=-1, keepdims=True)
            first = jnp.min(jnp.where(g == mx, lane, LANES), axis=-1, keepdims=True)
            pick = jnp.logical_and(lane == first, mx > -jnp.inf)
            keep = jnp.logical_or(keep, pick)
            g = jnp.where(pick, -jnp.inf, g)
        bias = jnp.where(keep, 0.0, NEG)
        q_aug.append(jnp.concatenate([q * (HEAD_DIM ** -0.5 * LOG2E), bias], axis=1).astype(BF16))

    causal = lax.broadcasted_iota(I32, (tq, tq), 1) <= lax.broadcasted_iota(I32, (tq, tq), 0)
    def finish(outs):
        for h in range(nh):
            o_ref[:, h * HEAD_DIM:(h + 1) * HEAD_DIM] = outs[h].astype(o_ref.dtype)

    _attend_unrolled(nblk, qt, finish, q_aug, [ksa_ref.at[h] for h in range(nh)],
                     [va_ref.at[h] for h in range(nh)], [s_ref.at[h] for h in range(nh)], tq, diag_mask=causal)


def _attn_b(proj, tab128, qg, kg, *, b, s):
    tq = MOBA_BLOCK
    nq = s // tq
    nh = B_HEADS
    kern = functools.partial(_attn_b_kernel, tq=tq, nblk=nq)
    small = lambda shape: pl.BlockSpec(shape, lambda bi, qt: (0,) * len(shape))
    qspec = lambda h: pl.BlockSpec((tq, LANES), lambda bi, qt: (bi * nq + qt, QB_C + h))
    kspec = lambda c0, h: pl.BlockSpec((s, LANES), lambda bi, qt: (bi, c0 + h))
    return pl.pallas_call(
        kern,
        grid=(b, nq),
        in_specs=([qspec(h) for h in range(nh)] + [kspec(KB_C, h) for h in range(nh)]
                  + [kspec(VB_C, h) for h in range(nh)]
                  + [pl.BlockSpec((3, tq, LANES), lambda bi, qt: (0, qt, 0)),
                     pl.BlockSpec((3, s, LANES), lambda bi, qt: (0, 0, 0)),
                     small((1, LANES)), small((1, LANES))]),
        out_specs=pl.BlockSpec((tq, nh * HEAD_DIM), lambda bi, qt: (bi * nq + qt, 0)),
        out_shape=jax.ShapeDtypeStruct((b * s, nh * HEAD_DIM), BF16),
        scratch_shapes=[pltpu.VMEM((nh, s, 2 * LANES), BF16), pltpu.VMEM((nh, s, 2 * LANES), BF16),
                        pltpu.VMEM((nh, LANES, LANES), F32), pltpu.VMEM((nh, nq, tq, tq), F32)],
        compiler_params=_cp(("parallel", "arbitrary")),
        name="attn_moba",
    )(*([proj] * (3 * nh)), tab128, tab128, qg, kg)


def _attn_c_kernel(q0_ref, q1_ref, q2_ref, q3_ref, q4_ref, qi_ref, kc_ref, vc_ref, kt_ref, wt_ref,
                   t128q_ref, t128k_ref, t64q_ref, t64k_ref, cqg_ref, ckg_ref, ikg_ref,
                   o_ref, kcs_ref, kis_ref, va_ref, keys_ref, s_ref, *, tq, tk, nq, n_sel):
    qt = pl.program_id(1)

    @pl.when(qt == 0)
    def _():
        k = _rope(_rms_full(kc_ref[...].astype(F32), ckg_ref[...]), t128k_ref, 16)
        kcs_ref[...] = k.astype(BF16)
        va_ref[:, :LANES] = vc_ref[...]
        va_ref[:, LANES:] = _ones_col(vc_ref.shape[0])
        t = kt_ref[...]
        lane = lax.broadcasted_iota(I32, t.shape, 1)
        ms = jnp.sum(jnp.where(lane < IDX_DIM, t * t, 0.0), axis=-1, keepdims=True) * (1.0 / IDX_DIM)
        ki = _rope(t * lax.rsqrt(ms + EPS) * ikg_ref[...], t64k_ref, 8)
        kis_ref[...] = (ki + pltpu.roll(ki, 64, 1)).astype(BF16)

    lane = lax.broadcasted_iota(I32, (tq, LANES), 1)
    heads = []
    for j in range(IDX_HEADS // 2):
        x = _rope(qi_ref[:, j * LANES:(j + 1) * LANES].astype(F32), t64q_ref, 8)
        heads.append(jnp.where(lane < 64, x, 0.0).astype(BF16))
        heads.append(jnp.where(lane < 64, 0.0, x).astype(BF16))
    hg = 4
    qi_groups = [jnp.concatenate(heads[g:g + hg], axis=0) for g in range(0, IDX_HEADS, hg)]
    wt = wt_ref[...] * ((IDX_HEADS ** -0.5) * (IDX_DIM ** -0.5))
    w_cols = [wt[:, IDX_DIM + h:IDX_DIM + h + 1] for h in range(IDX_HEADS)]

    row = qt * tq + lax.broadcasted_iota(I32, (tq, tk), 0)
    col0 = lax.broadcasted_iota(I32, (tq, tk), 1)
    nch = qt + 1

    def index_keys(c, _):
        off = pl.multiple_of(c * tk, tk)
        kchunk = kis_ref[pl.ds(off, tk), :]
        sc = jnp.zeros((tq, tk), F32)
        for g, qg in enumerate(qi_groups):
            lg = lax.dot_general(qg, kchunk, _NT, preferred_element_type=F32)
            for i in range(hg):
                sc = sc + jnp.maximum(lg[i * tq:(i + 1) * tq], 0.0) * w_cols[g * hg + i]
        keys_ref[c] = jnp.where(col0 + c * tk <= row, sc, -jnp.inf)
        return 0

    lax.fori_loop(0, nch, index_keys, 0)

    def as_float(k):
        return pltpu.bitcast(jnp.where(k < 0, k ^ jnp.int32(0x7FFFFFFF), k), F32)

    def count_ge(cand):
        cf = as_float(cand)

        def body(c, acc):
            a = jnp.where(keys_ref[c] >= cf, 1.0, 0.0)
            return acc + (a[:, :LANES] + a[:, LANES:])
        tot = lax.fori_loop(0, nch, body, jnp.zeros((tq, LANES), F32))
        return jnp.sum(tot, axis=-1, keepdims=True)

    lo = jnp.where(count_ge(jnp.zeros((tq, 1), I32)) >= n_sel, jnp.int32(0), jnp.int32(INT_MIN))

    def bit_step(i, lo):
        cand = lo + lax.shift_left(jnp.int32(1), 30 - i)
        return jnp.where(count_ge(cand) >= n_sel, cand, lo)

    lo = lax.fori_loop(0, 31, bit_step, lo)
    thr = as_float(jnp.maximum(lo, jnp.int32(INT_MIN + 0x00800000)))

    qs = []
    for q_ref in (q0_ref, q1_ref, q2_ref, q3_ref, q4_ref):
        q = _rope(_rms_full(q_ref[...].astype(F32), cqg_ref[...]), t128q_ref, 16)
        qs.append((q * (HEAD_DIM ** -0.5 * LOG2E)).astype(BF16))
    q_all = jnp.concatenate(qs, axis=0)
    rows = C_HEADS * tq

    def select(c, s):
        return jnp.where((keys_ref[c] >= thr)[None], s.reshape(C_HEADS, tq, tk), NEG).reshape(rows, tk)

    o, = _attend([q_all], [kcs_ref], [va_ref], [s_ref], qt, tk, mask_fn=select)
    for h in range(C_HEADS):
        o_ref[:, h * HEAD_DIM:(h + 1) * HEAD_DIM] = o[h * tq:(h + 1) * tq].astype(o_ref.dtype)


def _attn_c(proj, tail, tab128, tab64, cqg, ckg, ikg, *, b, s):
    tq, tk = 256, 256
    nq = s // tq
    n_sel = min(DSA_TOPK, s // 4)
    assert tq == tk
    kern = functools.partial(_attn_c_kernel, tq=tq, tk=tk, nq=nq, n_sel=n_sel)
    small = lambda shape: pl.BlockSpec(shape, lambda bi, qt: (0,) * len(shape))
    qspec = lambda h: pl.BlockSpec((tq, LANES), lambda bi, qt: (bi * nq + qt, QC_C + h))
    return pl.pallas_call(
        kern,
        grid=(b, nq),
        in_specs=[
            qspec(0), qspec(1), qspec(2), qspec(3), qspec(4),
            pl.BlockSpec((tq, IDX_HEADS * IDX_DIM), lambda bi, qt: (bi * nq + qt, QI_C // 8)),
            pl.BlockSpec((s, LANES), lambda bi, qt: (bi, KC_C)),
            pl.BlockSpec((s, LANES), lambda bi, qt: (bi, VC_C)),
            pl.BlockSpec((s, LANES), lambda bi, qt: (bi, 0)),
            pl.BlockSpec((tq, LANES), lambda bi, qt: (bi * nq + qt, 0)),
            pl.BlockSpec((3, tq, LANES), lambda bi, qt: (0, qt, 0)),
            pl.BlockSpec((3, s, LANES), lambda bi, qt: (0, 0, 0)),
            pl.BlockSpec((3, tq, LANES), lambda bi, qt: (0, qt, 0)),
            pl.BlockSpec((3, s, LANES), lambda bi, qt: (0, 0, 0)),
            small((1, LANES)), small((1, LANES)), small((1, LANES)),
        ],
        out_specs=pl.BlockSpec((tq, C_HEADS * HEAD_DIM), lambda bi, qt: (bi * nq + qt, 0)),
        out_shape=jax.ShapeDtypeStruct((b * s, C_HEADS * HEAD_DIM), BF16),
        scratch_shapes=[pltpu.VMEM((s, LANES), BF16), pltpu.VMEM((s, LANES), BF16),
                        pltpu.VMEM((s, 2 * LANES), BF16), pltpu.VMEM((s // tk, tq, tk), F32),
                        pltpu.VMEM((s // tk, C_HEADS * tq, tk), F32)],
        compiler_params=_cp(("parallel", "arbitrary")),
        name="attn_dsa",
    )(proj, proj, proj, proj, proj, proj, proj, proj, tail, tail,
      tab128, tab128, tab64, tab64, cqg, ckg, ikg)


def _out_router_kernel(x_ref, oa_ref, ob_ref, oc_ref, w_ref, g2_ref, wr_ref, br_ref,
                       x1_ref, t_ref, eid_ref, gate_ref):
    na = A_HEADS * HEAD_DIM
    nb = na + B_HEADS * HEAD_DIM
    x1 = (x_ref[...]
          + jnp.dot(oa_ref[...], w_ref[0:na, :], preferred_element_type=F32)
          + jnp.dot(ob_ref[...], w_ref[na:nb, :], preferred_element_type=F32)
          + jnp.dot(oc_ref[...], w_ref[nb:, :], preferred_element_type=F32))
    x1_ref[...] = x1
    t = _rms_full(x1, g2_ref[...])
    t_ref[...] = t

    lg = jnp.dot(t, wr_ref[...], preferred_element_type=F32, precision=lax.Precision.HIGHEST) + br_ref[...]
    lane = lax.broadcasted_iota(I32, lg.shape, 1)
    ninf = -jnp.inf
    gl = jnp.where(lane < N_GROUPS, lg, ninf)
    gm = jnp.max(gl, axis=-1, keepdims=True)
    ge = jnp.exp(gl - gm)
    g_prob = ge / jnp.sum(ge, axis=-1, keepdims=True)
    g_idx = jnp.min(jnp.where(gl == gm, lane, LANES), axis=-1, keepdims=True)
    g_w = jnp.sum(jnp.where(lane == g_idx, g_prob, 0.0), axis=-1, keepdims=True)

    e0 = N_GROUPS + g_idx * EXPERTS_PER_GROUP
    emask = jnp.logical_and(lane >= e0, lane < e0 + EXPERTS_PER_GROUP)
    el = jnp.where(emask, lg, ninf)
    em = jnp.max(el, axis=-1, keepdims=True)
    ee = jnp.exp(el - em)
    ep = jnp.where(emask, ee / jnp.sum(ee, axis=-1, keepdims=True), ninf)
    v1 = jnp.max(ep, axis=-1, keepdims=True)
    i1 = jnp.min(jnp.where(ep == v1, lane, LANES), axis=-1, keepdims=True)
    ep2 = jnp.where(lane == i1, ninf, ep)
    v2 = jnp.max(ep2, axis=-1, keepdims=True)
    i2 = jnp.min(jnp.where(ep2 == v2, lane, LANES), axis=-1, keepdims=True)
    den = v1 + v2
    eid_ref[...] = jnp.where(lane == 0, i1 - N_GROUPS, jnp.where(lane == 1, i2 - N_GROUPS, 0))
    gate_ref[...] = jnp.where(lane == 0, g_w * (v1 / den), jnp.where(lane == 1, g_w * (v2 / den), 0.0))


def _out_router(x, oa, ob, oc, w_out_bf, g2, w_router, b_router):
    t, d = x.shape
    tm = 256
    row = lambda c: pl.BlockSpec((tm, c), lambda i: (i, 0))
    full = lambda r, c: pl.BlockSpec((r, c), lambda i: (0, 0))
    return pl.pallas_call(
        _out_router_kernel,
        grid=(t // tm,),
        in_specs=[row(d), row(oa.shape[1]), row(ob.shape[1]), row(oc.shape[1]),
                  full(d, d), full(1, d), full(d, LANES), full(1, LANES)],
        out_specs=[row(d), row(d), row(LANES), row(LANES)],
        out_shape=[jax.ShapeDtypeStruct((t, d), F32), jax.ShapeDtypeStruct((t, d), F32),
                   jax.ShapeDtypeStruct((t, LANES), I32), jax.ShapeDtypeStruct((t, LANES), F32)],
        compiler_params=_cp(("parallel",)),
        name="out_proj_router",
    )(x, oa, ob, oc, w_out_bf, g2.reshape(1, d), w_router, b_router)


def _moe_kernel(texp_ref, nused_ref, rtok_ref, t_hbm, wg_ref, wu_ref, wd_ref,
                y_ref, xbuf, sem, wg_bf, wu_bf, wd_bf, *, tm):
    i = pl.program_id(0)
    nu = nused_ref[0]

    def gather(tile, slot):
        def body(r, _):
            tok = rtok_ref[tile * tm + r]
            pltpu.make_async_copy(t_hbm.at[pl.ds(tok, 1)], xbuf.at[slot, pl.ds(r, 1)], sem.at[slot]).start()
            return 0
        lax.fori_loop(0, tm, body, 0, unroll=8)

    @pl.when(i == 0)
    def _():
        gather(0, 0)

    @pl.when(i + 1 < nu)
    def _():
        gather(i + 1, (i + 1) % 2)

    @pl.when(i < nu)
    def _():
        slot = i % 2
        pltpu.make_async_copy(xbuf.at[slot], xbuf.at[slot], sem.at[slot]).wait()

        changed = jnp.logical_or(i == 0, texp_ref[i] != texp_ref[jnp.maximum(i - 1, 0)])

        @pl.when(changed)
        def _():
            wg_bf[...] = wg_ref[...].astype(BF16)
            wu_bf[...] = wu_ref[...].astype(BF16)
            wd_bf[...] = wd_ref[...].astype(BF16)

        x = xbuf[slot].astype(BF16)
        g = jnp.dot(x, wg_bf[...], preferred_element_type=F32)
        u = jnp.dot(x, wu_bf[...], preferred_element_type=F32)
        h = (g * jax.nn.sigmoid(g)) * u
        y_ref[...] = jnp.dot(h.astype(BF16), wd_bf[...], preferred_element_type=F32)

    @pl.when(i >= nu)
    def _():
        y_ref[...] = jnp.zeros_like(y_ref)


def _moe(t, tile_expert, n_used, row_token, w_gate, w_up, w_down, *, layer, n_tiles):
    tm = MOE_TILE
    d = t.shape[1]
    grid_spec = pltpu.PrefetchScalarGridSpec(
        num_scalar_prefetch=3,
        grid=(n_tiles,),
        in_specs=[
            pl.BlockSpec(memory_space=pl.ANY),
            pl.BlockSpec((None, None, d, EXPERT_FF), lambda i, te, nu, rt: (layer, te[i], 0, 0)),
            pl.BlockSpec((None, None, d, EXPERT_FF), lambda i, te, nu, rt: (layer, te[i], 0, 0)),
            pl.BlockSpec((None, None, EXPERT_FF, d), lambda i, te, nu, rt: (layer, te[i], 0, 0)),
        ],
        out_specs=pl.BlockSpec((tm, d), lambda i, te, nu, rt: (i, 0)),
        scratch_shapes=[pltpu.VMEM((2, tm, d), F32), pltpu.SemaphoreType.DMA((2,)),
                        pltpu.VMEM((d, EXPERT_FF), BF16), pltpu.VMEM((d, EXPERT_FF), BF16),
                        pltpu.VMEM((EXPERT_FF, d), BF16)],
    )
    return pl.pallas_call(
        functools.partial(_moe_kernel, tm=tm),
        grid_spec=grid_spec,
        out_shape=jax.ShapeDtypeStruct((n_tiles * tm, d), F32),
        compiler_params=_cp(("arbitrary",)),
        name="moe_experts",
    )(tile_expert, n_used, row_token, t, w_gate, w_up, w_down)


def _combine_kernel(pos_ref, x1_ref, gate_ref, y_hbm, o_ref, buf, sem, *, tc):
    i = pl.program_id(0)
    n = pl.num_programs(0)

    def gather(tile, slot):
        def body(r, _):
            a = (tile * tc + r) * 2
            pltpu.make_async_copy(y_hbm.at[pl.ds(pos_ref[a], 1)], buf.at[slot, 0, pl.ds(r, 1)], sem.at[slot]).start()
            pltpu.make_async_copy(y_hbm.at[pl.ds(pos_ref[a + 1], 1)], buf.at[slot, 1, pl.ds(r, 1)], sem.at[slot]).start()
            return 0
        lax.fori_loop(0, tc, body, 0, unroll=8)

    @pl.when(i == 0)
    def _():
        gather(0, 0)

    @pl.when(i + 1 < n)
    def _():
        gather(i + 1, (i + 1) % 2)

    slot = i % 2
    pltpu.make_async_copy(buf.at[slot], buf.at[slot], sem.at[slot]).wait()
    gt = gate_ref[...]
    o_ref[...] = x1_ref[...] + gt[:, 0:1] * buf[slot, 0] + gt[:, 1:2] * buf[slot, 1]


def _combine(pos, x1, gate, y):
    t, d = x1.shape
    tc = 256
    grid_spec = pltpu.PrefetchScalarGridSpec(
        num_scalar_prefetch=1,
        grid=(t // tc,),
        in_specs=[pl.BlockSpec((tc, d), lambda i, p: (i, 0)), pl.BlockSpec((tc, LANES), lambda i, p: (i, 0)),
                  pl.BlockSpec(memory_space=pl.ANY)],
        out_specs=pl.BlockSpec((tc, d), lambda i, p: (i, 0)),
        scratch_shapes=[pltpu.VMEM((2, 2, tc, d), F32), pltpu.SemaphoreType.DMA((2,))],
    )
    return pl.pallas_call(
        functools.partial(_combine_kernel, tc=tc),
        grid_spec=grid_spec,
        out_shape=jax.ShapeDtypeStruct((t, d), F32),
        compiler_params=_cp(("arbitrary",)),
        name="moe_combine",
    )(pos, x1, gate, y)


def _rope_lane_tables(seq, dim):
    rot = dim // ROPE_FRAC
    half = rot // 2
    inv = 1.0 / (ROPE_THETA ** (jnp.arange(0, rot, 2, dtype=F32) / rot))
    ang = jnp.arange(seq, dtype=F32)[:, None] * inv[None, :]
    cos, sin = jnp.cos(ang), jnp.sin(ang)
    z_half = jnp.zeros((seq, half), F32)
    z_rest = jnp.zeros((seq, dim - rot), F32)
    c = jnp.concatenate([cos, cos, jnp.ones((seq, dim - rot), F32)], axis=-1)
    sa = jnp.concatenate([-sin, z_half, z_rest], axis=-1)
    sb = jnp.concatenate([z_half, sin, z_rest], axis=-1)
    reps = LANES // dim
    return jnp.stack([jnp.tile(c, (1, reps)), jnp.tile(sa, (1, reps)), jnp.tile(sb, (1, reps))])


def _routing_plan(eid, n_tiles):
    tm = MOE_TILE
    e = eid[:, :2].reshape(-1)
    n_assign = e.shape[0]
    onehot = (e[:, None] == jnp.arange(N_EXPERTS, dtype=I32)[None, :]).astype(I32)
    csum = jnp.cumsum(onehot, axis=0)
    rank = jnp.take_along_axis(csum, e[:, None], axis=1)[:, 0] - 1
    counts = csum[-1]
    tiles_per = (counts + tm - 1) // tm
    tile_end = jnp.cumsum(tiles_per)
    tile_start = tile_end - tiles_per
    n_used = tile_end[-1]
    pos = tile_start[e] * tm + rank
    tile_ids = jnp.minimum(jnp.arange(n_tiles, dtype=I32), n_used - 1)
    tile_expert = jnp.sum((tile_end[None, :] <= tile_ids[:, None]).astype(I32), axis=1)
    row_token = jnp.zeros((n_tiles * tm,), I32).at[pos].set(jnp.arange(n_assign, dtype=I32) // 2)
    return tile_expert, n_used.reshape(1).astype(I32), row_token, pos.astype(I32)


def kernel(x, norm1_g, w_in, a_qn_g, a_kn_g, a_lambda, a_subln_g, b_qn_g, b_kn_g, c_qn_g, c_kn_g,
           idx_kn_g, w_out, norm2_g, w_group, b_group, w_expert, b_expert, w_gate, w_up, w_down):
    b, s, d = x.shape
    depth = w_in.shape[0]
    assert d == D_MODEL and s % MOBA_BLOCK == 0
    t = b * s
    n_tiles = (2 * t) // MOE_TILE + N_EXPERTS
    tab64 = _rope_lane_tables(s, A_QK)
    tab128 = _rope_lane_tables(s, HEAD_DIM)
    tile2 = lambda v: jnp.tile(v, 2).reshape(1, LANES)
    row = lambda v: v.reshape(1, LANES)

    w_main = w_in[:, :, :MAIN_COLS].astype(BF16)
    w_tail = jnp.pad(w_in[:, :, MAIN_COLS:], ((0, 0), (0, 0), (0, LANES - TAIL_COLS)))

    xf = x.reshape(t, d)
    for l in range(depth):
        lam_init = 0.8 - 0.6 * math.exp(-0.3 * l)
        h = _rmsnorm_bf16(xf, norm1_g[l])
        proj = _matmul(h, w_main, l, BF16, 1024, 1024, "in_proj")
        tail = _matmul(h, w_tail, l, F32, 1024, LANES, "in_proj_tail")

        oa = _attn_a(proj, tab64, tile2(a_qn_g[l]), tile2(a_kn_g[l]), a_lambda[l], row(a_subln_g[l]),
                     b=b, s=s, lam_init=lam_init)
        ob = _attn_b(proj, tab128, row(b_qn_g[l]), row(b_kn_g[l]), b=b, s=s)
        ikg = jnp.pad(idx_kn_g[l], (0, LANES - IDX_DIM)).reshape(1, LANES)
        oc = _attn_c(proj, tail, tab128, tab64, row(c_qn_g[l]), row(c_kn_g[l]), ikg, b=b, s=s)

        w_router = jnp.pad(jnp.concatenate([w_group[l], w_expert[l]], axis=1),
                           ((0, 0), (0, LANES - N_GROUPS - N_EXPERTS)))
        b_router = jnp.pad(jnp.concatenate([b_group[l], b_expert[l]]),
                           (0, LANES - N_GROUPS - N_EXPERTS)).reshape(1, LANES)
        x1, tn, eid, gate = _out_router(xf, oa, ob, oc, _cast_bf16(w_out, l, d), norm2_g[l], w_router, b_router)

        tile_expert, n_used, row_token, pos = _routing_plan(eid, n_tiles)
        y = _moe(tn, tile_expert, n_used, row_token, w_gate, w_up, w_down, layer=l, n_tiles=n_tiles)
        xf = _combine(pos, x1, gate, y)
    return xf.reshape(b, s, d)
```

```python
import functools
import math

import jax
import jax.numpy as jnp
from jax import lax
from jax.experimental import pallas as pl
from jax.experimental.pallas import tpu as pltpu

F32 = jnp.float32
BF16 = jnp.bfloat16
I32 = jnp.int32

D_MODEL = 2048
HEAD_DIM = 128
A_HEADS = 6
A_QK = 64
B_HEADS = 5
MOBA_BLOCK = 256
MOBA_TOPK = 3
C_HEADS = 5
IDX_HEADS = 16
IDX_DIM = 64
DSA_TOPK = 256
ROPE_THETA = 500000.0
ROPE_FRAC = 4
EPS = 1e-6
N_GROUPS = 4
EXPERTS_PER_GROUP = 8
N_EXPERTS = N_GROUPS * EXPERTS_PER_GROUP
EXPERT_FF = 512

LANES = 128
MAIN_COLS = 6144
TAIL_COLS = 80
QA_C, KA_C, VA_C = 0, 6, 12
QB_C, KB_C, VB_C = 18, 23, 28
QC_C, KC_C, VC_C = 33, 38, 39
QI_C = 40
NEG = -1e30
LOG2E = 1.4426950408889634
INT_MIN = -(2 ** 31)
MOE_TILE = 256
VMEM_LIMIT = 56 * 1024 * 1024

_NT = (((1,), (1,)), ((), ()))


def _cp(sem, vmem=VMEM_LIMIT):
    return pltpu.CompilerParams(dimension_semantics=sem, vmem_limit_bytes=vmem)


def _rms_full(x, g):
    ms = jnp.mean(x * x, axis=-1, keepdims=True)
    return x * lax.rsqrt(ms + EPS) * g


def _rms_halves(x, g):
    lane = lax.broadcasted_iota(I32, x.shape, 1)
    lo = lane < 64
    x2 = x * x
    s_lo = jnp.sum(jnp.where(lo, x2, 0.0), axis=-1, keepdims=True)
    s_hi = jnp.sum(jnp.where(lo, 0.0, x2), axis=-1, keepdims=True)
    r = jnp.where(lo, lax.rsqrt(s_lo * (1.0 / 64) + EPS), lax.rsqrt(s_hi * (1.0 / 64) + EPS))
    return x * r * g


def _rope(x, tab_ref, half):
    c = tab_ref[0]
    sa = tab_ref[1]
    sb = tab_ref[2]
    return x * c + pltpu.roll(x, LANES - half, 1) * sa + pltpu.roll(x, half, 1) * sb


def _ones_col(rows):
    lane = lax.broadcasted_iota(I32, (rows, LANES), 1)
    return jnp.where(lane == 0, 1.0, 0.0).astype(BF16)


def _lane_max2(mrun, s):
    return jnp.maximum(mrun, jnp.maximum(s[:, :LANES], s[:, LANES:]))


def _attend(qs, k_refs, va_refs, s_refs, qt, tk, *, diag_mask=None, mask_fn=None):
    n = len(qs)
    rows = qs[0].shape[0]
    static = isinstance(qt, int)

    def loop(lo, hi, body, init):
        if not static:
            return lax.fori_loop(lo, hi, body, init)
        for c in range(lo, hi):
            init = body(c, init)
        return init

    def chunk(ref, c):
        if static:
            return ref[c * tk:(c + 1) * tk, :]
        return ref[pl.ds(pl.multiple_of(c * tk, tk), tk), :]

    def scores(h, c):
        s = lax.dot_general(qs[h], chunk(k_refs[h], c), _NT, preferred_element_type=F32)
        return s if mask_fn is None else mask_fn(c, s)

    def first(c, mruns):
        out = []
        for h in range(n):
            s = scores(h, c)
            s_refs[h][c] = s
            out.append(_lane_max2(mruns[h], s))
        return tuple(out)

    init = tuple(jnp.full((rows, LANES), NEG, F32) for _ in range(n))
    if diag_mask is None:
        mruns = loop(0, qt + 1, first, init)
    else:
        mruns = list(loop(0, qt, first, init))
        for h in range(n):
            s = jnp.where(diag_mask, scores(h, qt), NEG)
            s_refs[h][qt] = s
            mruns[h] = _lane_max2(mruns[h], s)
    ms = [jnp.max(mr, axis=-1, keepdims=True) for mr in mruns]

    def second(c, accs):
        return tuple(
            accs[h] + jnp.dot(jnp.exp2(s_refs[h][c] - ms[h]).astype(BF16), chunk(va_refs[h], c),
                              preferred_element_type=F32)
            for h in range(n))

    accs = loop(0, qt + 1, second, tuple(jnp.zeros((rows, 2 * LANES), F32) for _ in range(n)))
    return [a[:, :LANES] / a[:, LANES:LANES + 1] for a in accs]


def _attend_unrolled(nq, qt, finish, qs, k_refs, va_refs, s_refs, tk, **kwargs):
    for j in range(nq):
        pl.when(qt == j)(lambda j=j: finish(_attend(qs, k_refs, va_refs, s_refs, j, tk, **kwargs)))


def _norm_kernel(x_ref, g_ref, o_ref):
    o_ref[...] = _rms_full(x_ref[...], g_ref[...]).astype(o_ref.dtype)


def _rmsnorm_bf16(x, g):
    t, d = x.shape
    tm = 512
    return pl.pallas_call(
        _norm_kernel,
        grid=(t // tm,),
        in_specs=[pl.BlockSpec((tm, d), lambda i: (i, 0)), pl.BlockSpec((1, d), lambda i: (0, 0))],
        out_specs=pl.BlockSpec((tm, d), lambda i: (i, 0)),
        out_shape=jax.ShapeDtypeStruct((t, d), BF16),
        compiler_params=_cp(("parallel",)),
        name="rmsnorm",
    )(x, g.reshape(1, d))


def _cast_kernel(x_ref, o_ref):
    o_ref[...] = x_ref[...].astype(o_ref.dtype)


def _cast_bf16(w, layer, ncols):
    r = w.shape[1]
    tr, tc = 256, min(ncols, 1024)
    return pl.pallas_call(
        _cast_kernel,
        grid=(r // tr, ncols // tc),
        in_specs=[pl.BlockSpec((None, tr, tc), lambda i, j: (layer, i, j))],
        out_specs=pl.BlockSpec((tr, tc), lambda i, j: (i, j)),
        out_shape=jax.ShapeDtypeStruct((r, ncols), BF16),
        compiler_params=_cp(("parallel", "parallel")),
        name="cast_bf16",
    )(w)


def _matmul_kernel(x_ref, w_ref, o_ref):
    w = w_ref[...].astype(x_ref.dtype)
    o_ref[...] = jnp.dot(x_ref[...], w, preferred_element_type=F32).astype(o_ref.dtype)


def _matmul(x, w, layer, out_dtype, tm, tn, name):
    m, k = x.shape
    n = w.shape[2]
    return pl.pallas_call(
        _matmul_kernel,
        grid=(n // tn, m // tm),
        in_specs=[pl.BlockSpec((tm, k), lambda j, i: (i, 0)),
                  pl.BlockSpec((None, k, tn), lambda j, i: (layer, 0, j))],
        out_specs=pl.BlockSpec((tm, tn), lambda j, i: (i, j)),
        out_shape=jax.ShapeDtypeStruct((m, n), out_dtype),
        compiler_params=_cp(("parallel", "parallel")),
        name=name,
    )(x, w)


def _attn_a_kernel(*refs, tq, nq, nh, lam_init):
    q_refs, k_refs, v_refs = refs[0:nh], refs[nh:2 * nh], refs[2 * nh:3 * nh]
    tabq_ref, tabk_ref, qg_ref, kg_ref, lam_ref, sg_ref, o_ref, ks_ref, va_ref, s_ref = refs[3 * nh:]
    qt = pl.program_id(2)

    @pl.when(qt == 0)
    def _():
        ones = _ones_col(ks_ref.shape[1])
        for h in range(nh):
            k = _rms_halves(k_refs[h][...].astype(F32), kg_ref[...])
            ks_ref[h] = _rope(k, tabk_ref, 8).astype(BF16)
            va_ref[h, :, :LANES] = v_refs[h][...]
            va_ref[h, :, LANES:] = ones

    lane = lax.broadcasted_iota(I32, (tq, LANES), 1)
    qs = []
    for h in range(nh):
        q = _rms_halves(q_refs[h][...].astype(F32), qg_ref[...])
        q = _rope(q, tabq_ref, 8) * (A_QK ** -0.5 * LOG2E)
        qs.append(jnp.concatenate([jnp.where(lane < 64, q, 0.0), jnp.where(lane < 64, 0.0, q)],
                                  axis=0).astype(BF16))

    lp = lam_ref[...]
    lam = (jnp.exp(jnp.sum(lp[0:1] * lp[1:2], axis=-1, keepdims=True))
           - jnp.exp(jnp.sum(lp[2:3] * lp[3:4], axis=-1, keepdims=True)) + lam_init)

    row = lax.broadcasted_iota(I32, (2 * tq, tq), 0)
    row = jnp.where(row >= tq, row - tq, row)
    causal = lax.broadcasted_iota(I32, (2 * tq, tq), 1) <= row
    def finish(outs):
        for h in range(nh):
            o = outs[h][:tq] - lam * outs[h][tq:]
            o = _rms_full(o, sg_ref[...]) * (1.0 - lam_init)
            o_ref[:, h * HEAD_DIM:(h + 1) * HEAD_DIM] = o.astype(o_ref.dtype)

    _attend_unrolled(nq, qt, finish, qs, [ks_ref.at[h] for h in range(nh)], [va_ref.at[h] for h in range(nh)],
                     [s_ref.at[h] for h in range(nh)], tq, diag_mask=causal)


def _attn_a(proj, tab64, qg, kg, lam_p, sg, *, b, s, lam_init):
    tq = 256
    nq = s // tq
    nh = 1
    ng = A_HEADS // nh
    kern = functools.partial(_attn_a_kernel, tq=tq, nq=nq, nh=nh, lam_init=lam_init)
    small = lambda shape: pl.BlockSpec(shape, lambda bi, g, qt: (0,) * len(shape))
    qspec = lambda h: pl.BlockSpec((tq, LANES), lambda bi, g, qt: (bi * nq + qt, QA_C + g * nh + h))
    kspec = lambda c0, h: pl.BlockSpec((s, LANES), lambda bi, g, qt: (bi, c0 + g * nh + h))
    return pl.pallas_call(
        kern,
        grid=(b, ng, nq),
        in_specs=([qspec(h) for h in range(nh)] + [kspec(KA_C, h) for h in range(nh)]
                  + [kspec(VA_C, h) for h in range(nh)]
                  + [pl.BlockSpec((3, tq, LANES), lambda bi, g, qt: (0, qt, 0)),
                     pl.BlockSpec((3, s, LANES), lambda bi, g, qt: (0, 0, 0)),
                     small((1, LANES)), small((1, LANES)), small((4, A_QK)), small((1, LANES))]),
        out_specs=pl.BlockSpec((tq, nh * HEAD_DIM), lambda bi, g, qt: (bi * nq + qt, g)),
        out_shape=jax.ShapeDtypeStruct((b * s, A_HEADS * HEAD_DIM), BF16),
        scratch_shapes=[pltpu.VMEM((nh, s, LANES), BF16), pltpu.VMEM((nh, s, 2 * LANES), BF16),
                        pltpu.VMEM((nh, nq, 2 * tq, tq), F32)],
        compiler_params=_cp(("parallel", "parallel", "arbitrary")),
        name="attn_diff",
    )(*([proj] * (3 * nh)), tab64, tab64, qg, kg, lam_p, sg)


def _attn_b_kernel(*refs, tq, nblk):
    nh = B_HEADS
    q_refs, k_refs, v_refs = refs[0:nh], refs[nh:2 * nh], refs[2 * nh:3 * nh]
    tabq_ref, tabk_ref, qg_ref, kg_ref, o_ref, ksa_ref, va_ref, kmean_ref, s_ref = refs[3 * nh:]
    qt = pl.program_id(1)
    s_len = k_refs[0].shape[0]

    @pl.when(qt == 0)
    def _():
        blk = lax.shift_right_logical(lax.broadcasted_iota(I32, (s_len, LANES), 0), MOBA_BLOCK.bit_length() - 1)
        onehot = jnp.where(lax.broadcasted_iota(I32, (s_len, LANES), 1) == blk, 1.0, 0.0).astype(BF16)
        ones = _ones_col(s_len)
        for h in range(nh):
            k = _rope(_rms_full(k_refs[h][...].astype(F32), kg_ref[...]), tabk_ref, 16)
            ksa_ref[h, :, :LANES] = k.astype(BF16)
            ksa_ref[h, :, LANES:] = onehot
            kmean_ref[h] = jnp.zeros((LANES, LANES), F32)
            kmean_ref[h, 0:nblk, :] = jnp.mean(k.reshape(nblk, MOBA_BLOCK, HEAD_DIM), axis=1)
            va_ref[h, :, :LANES] = v_refs[h][...]
            va_ref[h, :, LANES:] = ones

    nbp = -(-nblk // 8) * 8
    blk = lax.broadcasted_iota(I32, (nbp, tq), 0)
    q_aug = []
    for h in range(nh):
        q = _rope(_rms_full(q_refs[h][...].astype(F32), qg_ref[...]), tabq_ref, 16)
        gate = lax.dot_general(kmean_ref[h, 0:nbp, :], q, _NT, preferred_element_type=F32,
                               precision=lax.Precision.HIGHEST)
        g = jnp.where(blk < qt, gate, -jnp.inf)
        keep = blk == qt
        for _ in range(MOBA_TOPK):
            mx = jnp.max(g, axis=0, keepdims=True)
            first = jnp.min(jnp.where(g == mx, blk, nbp), axis=0, keepdims=True)
            pick = jnp.logical_and(blk == first, mx > -jnp.inf)
            keep = jnp.logical_or(keep, pick)
            g = jnp.where(pick, -jnp.inf, g)
        bias_t = jnp.concatenate([jnp.where(keep, 0.0, NEG), jnp.zeros((LANES - nbp, tq), F32)], axis=0)
        q_aug.append(jnp.concatenate([q * (HEAD_DIM ** -0.5 * LOG2E), bias_t.T], axis=1).astype(BF16))

    causal = lax.broadcasted_iota(I32, (tq, tq), 1) <= lax.broadcasted_iota(I32, (tq, tq), 0)
    def finish(outs):
        for h in range(nh):
            o_ref[:, h * HEAD_DIM:(h + 1) * HEAD_DIM] = outs[h].astype(o_ref.dtype)

    _attend_unrolled(nblk, qt, finish, q_aug, [ksa_ref.at[h] for h in range(nh)],
                     [va_ref.at[h] for h in range(nh)], [s_ref.at[h] for h in range(nh)], tq, diag_mask=causal)


def _attn_b(proj, tab128, qg, kg, *, b, s):
    tq = MOBA_BLOCK
    nq = s // tq
    nh = B_HEADS
    kern = functools.partial(_attn_b_kernel, tq=tq, nblk=nq)
    small = lambda shape: pl.BlockSpec(shape, lambda bi, qt: (0,) * len(shape))
    qspec = lambda h: pl.BlockSpec((tq, LANES), lambda bi, qt: (bi * nq + qt, QB_C + h))
    kspec = lambda c0, h: pl.BlockSpec((s, LANES), lambda bi, qt: (bi, c0 + h))
    return pl.pallas_call(
        kern,
        grid=(b, nq),
        in_specs=([qspec(h) for h in range(nh)] + [kspec(KB_C, h) for h in range(nh)]
                  + [kspec(VB_C, h) for h in range(nh)]
                  + [pl.BlockSpec((3, tq, LANES), lambda bi, qt: (0, qt, 0)),
                     pl.BlockSpec((3, s, LANES), lambda bi, qt: (0, 0, 0)),
                     small((1, LANES)), small((1, LANES))]),
        out_specs=pl.BlockSpec((tq, nh * HEAD_DIM), lambda bi, qt: (bi * nq + qt, 0)),
        out_shape=jax.ShapeDtypeStruct((b * s, nh * HEAD_DIM), BF16),
        scratch_shapes=[pltpu.VMEM((nh, s, 2 * LANES), BF16), pltpu.VMEM((nh, s, 2 * LANES), BF16),
                        pltpu.VMEM((nh, LANES, LANES), F32), pltpu.VMEM((nh, nq, tq, tq), F32)],
        compiler_params=_cp(("parallel", "arbitrary")),
        name="attn_moba",
    )(*([proj] * (3 * nh)), tab128, tab128, qg, kg)


def _attn_c_kernel(q0_ref, q1_ref, q2_ref, q3_ref, q4_ref, qi_ref, kc_ref, vc_ref, kt_ref, wt_ref,
                   t128q_ref, t128k_ref, t64q_ref, t64k_ref, cqg_ref, ckg_ref, ikg_ref,
                   o_ref, kcs_ref, kis_ref, va_ref, keys_ref, s_ref, *, tq, tk, nq, n_sel):
    qt = pl.program_id(1)

    @pl.when(qt == 0)
    def _():
        k = _rope(_rms_full(kc_ref[...].astype(F32), ckg_ref[...]), t128k_ref, 16)
        kcs_ref[...] = k.astype(BF16)
        va_ref[:, :LANES] = vc_ref[...]
        va_ref[:, LANES:] = _ones_col(vc_ref.shape[0])
        t = kt_ref[...]
        lane = lax.broadcasted_iota(I32, t.shape, 1)
        ms = jnp.sum(jnp.where(lane < IDX_DIM, t * t, 0.0), axis=-1, keepdims=True) * (1.0 / IDX_DIM)
        ki = _rope(t * lax.rsqrt(ms + EPS) * ikg_ref[...], t64k_ref, 8)
        kis_ref[...] = (ki + pltpu.roll(ki, 64, 1)).astype(BF16)

    lane = lax.broadcasted_iota(I32, (tq, LANES), 1)
    heads = []
    for j in range(IDX_HEADS // 2):
        x = _rope(qi_ref[:, j * LANES:(j + 1) * LANES].astype(F32), t64q_ref, 8)
        heads.append(jnp.where(lane < 64, x, 0.0).astype(BF16))
        heads.append(jnp.where(lane < 64, 0.0, x).astype(BF16))
    hg = 4
    qi_groups = [jnp.concatenate(heads[g:g + hg], axis=0) for g in range(0, IDX_HEADS, hg)]
    wt = wt_ref[...] * ((IDX_HEADS ** -0.5) * (IDX_DIM ** -0.5))
    w_cols = [wt[:, IDX_DIM + h:IDX_DIM + h + 1] for h in range(IDX_HEADS)]

    row = qt * tq + lax.broadcasted_iota(I32, (tq, tk), 0)
    col0 = lax.broadcasted_iota(I32, (tq, tk), 1)
    nch = (qt + 1) * (tq // tk)

    def index_keys(c, _):
        off = pl.multiple_of(c * tk, tk)
        kchunk = kis_ref[pl.ds(off, tk), :]
        sc = jnp.zeros((tq, tk), F32)
        for g, qg in enumerate(qi_groups):
            lg = lax.dot_general(qg, kchunk, _NT, preferred_element_type=F32)
            for i in range(hg):
                sc = sc + jnp.maximum(lg[i * tq:(i + 1) * tq], 0.0) * w_cols[g * hg + i]
        keys_ref[c] = jnp.where(col0 + c * tk <= row, sc, -jnp.inf)
        return 0

    lax.fori_loop(0, nch, index_keys, 0)

    def as_float(k):
        return pltpu.bitcast(jnp.where(k < 0, k ^ jnp.int32(0x7FFFFFFF), k), F32)

    def count_ge(cand):
        cf = as_float(cand)

        def body(c, acc):
            a = jnp.where(keys_ref[c] >= cf, 1.0, 0.0)
            return acc + (a[:, :LANES] + a[:, LANES:])
        tot = lax.fori_loop(0, nch, body, jnp.zeros((tq, LANES), F32))
        return jnp.sum(tot, axis=-1, keepdims=True)

    lo = jnp.where(count_ge(jnp.zeros((tq, 1), I32)) >= n_sel, jnp.int32(0), jnp.int32(INT_MIN))

    def bit_step(i, lo):
        cand = lo + lax.shift_left(jnp.int32(1), 30 - i)
        return jnp.where(count_ge(cand) >= n_sel, cand, lo)

    lo = lax.fori_loop(0, 31, bit_step, lo)
    thr = as_float(jnp.maximum(lo, jnp.int32(INT_MIN + 0x00800000)))

    qs = []
    for q_ref in (q0_ref, q1_ref, q2_ref, q3_ref, q4_ref):
        q = _rope(_rms_full(q_ref[...].astype(F32), cqg_ref[...]), t128q_ref, 16)
        qs.append((q * (HEAD_DIM ** -0.5 * LOG2E)).astype(BF16))
    q_all = jnp.concatenate(qs, axis=0)
    rows = C_HEADS * tq

    def select(c, s):
        return jnp.where((keys_ref[c] >= thr)[None], s.reshape(C_HEADS, tq, tk), NEG).reshape(rows, tk)

    o, = _attend([q_all], [kcs_ref], [va_ref], [s_ref], nch - 1, tk, mask_fn=select)
    for h in range(C_HEADS):
        o_ref[:, h * HEAD_DIM:(h + 1) * HEAD_DIM] = o[h * tq:(h + 1) * tq].astype(o_ref.dtype)


def _attn_c(proj, tail, tab128, tab64, cqg, ckg, ikg, *, b, s):
    tq, tk = 512, 256
    nq = s // tq
    n_sel = min(DSA_TOPK, s // 4)
    assert tq % tk == 0 and s % tq == 0
    kern = functools.partial(_attn_c_kernel, tq=tq, tk=tk, nq=nq, n_sel=n_sel)
    small = lambda shape: pl.BlockSpec(shape, lambda bi, qt: (0,) * len(shape))
    qspec = lambda h: pl.BlockSpec((tq, LANES), lambda bi, qt: (bi * nq + qt, QC_C + h))
    return pl.pallas_call(
        kern,
        grid=(b, nq),
        in_specs=[
            qspec(0), qspec(1), qspec(2), qspec(3), qspec(4),
            pl.BlockSpec((tq, IDX_HEADS * IDX_DIM), lambda bi, qt: (bi * nq + qt, QI_C // 8)),
            pl.BlockSpec((s, LANES), lambda bi, qt: (bi, KC_C)),
            pl.BlockSpec((s, LANES), lambda bi, qt: (bi, VC_C)),
            pl.BlockSpec((s, LANES), lambda bi, qt: (bi, 0)),
            pl.BlockSpec((tq, LANES), lambda bi, qt: (bi * nq + qt, 0)),
            pl.BlockSpec((3, tq, LANES), lambda bi, qt: (0, qt, 0)),
            pl.BlockSpec((3, s, LANES), lambda bi, qt: (0, 0, 0)),
            pl.BlockSpec((3, tq, LANES), lambda bi, qt: (0, qt, 0)),
            pl.BlockSpec((3, s, LANES), lambda bi, qt: (0, 0, 0)),
            small((1, LANES)), small((1, LANES)), small((1, LANES)),
        ],
        out_specs=pl.BlockSpec((tq, C_HEADS * HEAD_DIM), lambda bi, qt: (bi * nq + qt, 0)),
        out_shape=jax.ShapeDtypeStruct((b * s, C_HEADS * HEAD_DIM), BF16),
        scratch_shapes=[pltpu.VMEM((s, LANES), BF16), pltpu.VMEM((s, LANES), BF16),
                        pltpu.VMEM((s, 2 * LANES), BF16), pltpu.VMEM((s // tk, tq, tk), F32),
                        pltpu.VMEM((s // tk, C_HEADS * tq, tk), F32)],
        compiler_params=_cp(("parallel", "arbitrary")),
        name="attn_dsa",
    )(proj, proj, proj, proj, proj, proj, proj, proj, tail, tail,
      tab128, tab128, tab64, tab64, cqg, ckg, ikg)


def _out_router_kernel(x_ref, oa_ref, ob_ref, oc_ref, w_ref, g2_ref, wr_ref, br_ref,
                       x1_ref, t_ref, eid_ref, gate_ref):
    na = A_HEADS * HEAD_DIM
    nb = na + B_HEADS * HEAD_DIM
    x1 = (x_ref[...]
          + jnp.dot(oa_ref[...], w_ref[0:na, :], preferred_element_type=F32)
          + jnp.dot(ob_ref[...], w_ref[na:nb, :], preferred_element_type=F32)
          + jnp.dot(oc_ref[...], w_ref[nb:, :], preferred_element_type=F32))
    x1_ref[...] = x1
    t = _rms_full(x1, g2_ref[...])
    t_ref[...] = t

    lg = jnp.dot(t, wr_ref[...], preferred_element_type=F32, precision=lax.Precision.HIGHEST) + br_ref[...]
    lane = lax.broadcasted_iota(I32, lg.shape, 1)
    ninf = -jnp.inf
    gl = jnp.where(lane < N_GROUPS, lg, ninf)
    gm = jnp.max(gl, axis=-1, keepdims=True)
    ge = jnp.exp(gl - gm)
    g_prob = ge / jnp.sum(ge, axis=-1, keepdims=True)
    g_idx = jnp.min(jnp.where(gl == gm, lane, LANES), axis=-1, keepdims=True)
    g_w = jnp.sum(jnp.where(lane == g_idx, g_prob, 0.0), axis=-1, keepdims=True)

    e0 = N_GROUPS + g_idx * EXPERTS_PER_GROUP
    emask = jnp.logical_and(lane >= e0, lane < e0 + EXPERTS_PER_GROUP)
    el = jnp.where(emask, lg, ninf)
    em = jnp.max(el, axis=-1, keepdims=True)
    ee = jnp.exp(el - em)
    ep = jnp.where(emask, ee / jnp.sum(ee, axis=-1, keepdims=True), ninf)
    v1 = jnp.max(ep, axis=-1, keepdims=True)
    i1 = jnp.min(jnp.where(ep == v1, lane, LANES), axis=-1, keepdims=True)
    ep2 = jnp.where(lane == i1, ninf, ep)
    v2 = jnp.max(ep2, axis=-1, keepdims=True)
    i2 = jnp.min(jnp.where(ep2 == v2, lane, LANES), axis=-1, keepdims=True)
    den = v1 + v2
    eid_ref[...] = jnp.where(lane == 0, i1 - N_GROUPS, jnp.where(lane == 1, i2 - N_GROUPS, 0))
    gate_ref[...] = jnp.where(lane == 0, g_w * (v1 / den), jnp.where(lane == 1, g_w * (v2 / den), 0.0))


def _out_router(x, oa, ob, oc, w_out_bf, g2, w_router, b_router):
    t, d = x.shape
    tm = 256
    row = lambda c: pl.BlockSpec((tm, c), lambda i: (i, 0))
    full = lambda r, c: pl.BlockSpec((r, c), lambda i: (0, 0))
    return pl.pallas_call(
        _out_router_kernel,
        grid=(t // tm,),
        in_specs=[row(d), row(oa.shape[1]), row(ob.shape[1]), row(oc.shape[1]),
                  full(d, d), full(1, d), full(d, LANES), full(1, LANES)],
        out_specs=[row(d), row(d), row(LANES), row(LANES)],
        out_shape=[jax.ShapeDtypeStruct((t, d), F32), jax.ShapeDtypeStruct((t, d), F32),
                   jax.ShapeDtypeStruct((t, LANES), I32), jax.ShapeDtypeStruct((t, LANES), F32)],
        compiler_params=_cp(("parallel",)),
        name="out_proj_router",
    )(x, oa, ob, oc, w_out_bf, g2.reshape(1, d), w_router, b_router)


def _moe_kernel(texp_ref, nused_ref, rtok_ref, t_hbm, wg_ref, wu_ref, wd_ref,
                y_ref, xbuf, sem, wg_bf, wu_bf, wd_bf, *, tm):
    i = pl.program_id(0)
    nu = nused_ref[0]

    def gather(tile, slot):
        def body(r, _):
            tok = rtok_ref[tile * tm + r]
            pltpu.make_async_copy(t_hbm.at[pl.ds(tok, 1)], xbuf.at[slot, pl.ds(r, 1)], sem.at[slot]).start()
            return 0
        lax.fori_loop(0, tm, body, 0, unroll=8)

    @pl.when(i == 0)
    def _():
        gather(0, 0)

    @pl.when(i + 1 < nu)
    def _():
        gather(i + 1, (i + 1) % 2)

    @pl.when(i < nu)
    def _():
        slot = i % 2
        pltpu.make_async_copy(xbuf.at[slot], xbuf.at[slot], sem.at[slot]).wait()

        changed = jnp.logical_or(i == 0, texp_ref[i] != texp_ref[jnp.maximum(i - 1, 0)])

        @pl.when(changed)
        def _():
            wg_bf[...] = wg_ref[...].astype(BF16)
            wu_bf[...] = wu_ref[...].astype(BF16)
            wd_bf[...] = wd_ref[...].astype(BF16)

        x = xbuf[slot].astype(BF16)
        g = jnp.dot(x, wg_bf[...], preferred_element_type=F32)
        u = jnp.dot(x, wu_bf[...], preferred_element_type=F32)
        h = (g * jax.nn.sigmoid(g)) * u
        y_ref[...] = jnp.dot(h.astype(BF16), wd_bf[...], preferred_element_type=F32)

    @pl.when(i >= nu)
    def _():
        y_ref[...] = jnp.zeros_like(y_ref)


def _moe(t, tile_expert, n_used, row_token, w_gate, w_up, w_down, *, layer, n_tiles):
    tm = MOE_TILE
    d = t.shape[1]
    grid_spec = pltpu.PrefetchScalarGridSpec(
        num_scalar_prefetch=3,
        grid=(n_tiles,),
        in_specs=[
            pl.BlockSpec(memory_space=pl.ANY),
            pl.BlockSpec((None, None, d, EXPERT_FF), lambda i, te, nu, rt: (layer, te[i], 0, 0)),
            pl.BlockSpec((None, None, d, EXPERT_FF), lambda i, te, nu, rt: (layer, te[i], 0, 0)),
            pl.BlockSpec((None, None, EXPERT_FF, d), lambda i, te, nu, rt: (layer, te[i], 0, 0)),
        ],
        out_specs=pl.BlockSpec((tm, d), lambda i, te, nu, rt: (i, 0)),
        scratch_shapes=[pltpu.VMEM((2, tm, d), F32), pltpu.SemaphoreType.DMA((2,)),
                        pltpu.VMEM((d, EXPERT_FF), BF16), pltpu.VMEM((d, EXPERT_FF), BF16),
                        pltpu.VMEM((EXPERT_FF, d), BF16)],
    )
    return pl.pallas_call(
        functools.partial(_moe_kernel, tm=tm),
        grid_spec=grid_spec,
        out_shape=jax.ShapeDtypeStruct((n_tiles * tm, d), F32),
        compiler_params=_cp(("arbitrary",)),
        name="moe_experts",
    )(tile_expert, n_used, row_token, t, w_gate, w_up, w_down)


def _combine_kernel(pos_ref, x1_ref, gate_ref, y_hbm, o_ref, buf, sem, *, tc):
    i = pl.program_id(0)
    n = pl.num_programs(0)

    def gather(tile, slot):
        def body(r, _):
            a = (tile * tc + r) * 2
            pltpu.make_async_copy(y_hbm.at[pl.ds(pos_ref[a], 1)], buf.at[slot, 0, pl.ds(r, 1)], sem.at[slot]).start()
            pltpu.make_async_copy(y_hbm.at[pl.ds(pos_ref[a + 1], 1)], buf.at[slot, 1, pl.ds(r, 1)], sem.at[slot]).start()
            return 0
        lax.fori_loop(0, tc, body, 0, unroll=8)

    @pl.when(i == 0)
    def _():
        gather(0, 0)

    @pl.when(i + 1 < n)
    def _():
        gather(i + 1, (i + 1) % 2)

    slot = i % 2
    pltpu.make_async_copy(buf.at[slot], buf.at[slot], sem.at[slot]).wait()
    gt = gate_ref[...]
    o_ref[...] = x1_ref[...] + gt[:, 0:1] * buf[slot, 0] + gt[:, 1:2] * buf[slot, 1]


def _combine(pos, x1, gate, y):
    t, d = x1.shape
    tc = 256
    grid_spec = pltpu.PrefetchScalarGridSpec(
        num_scalar_prefetch=1,
        grid=(t // tc,),
        in_specs=[pl.BlockSpec((tc, d), lambda i, p: (i, 0)), pl.BlockSpec((tc, LANES), lambda i, p: (i, 0)),
                  pl.BlockSpec(memory_space=pl.ANY)],
        out_specs=pl.BlockSpec((tc, d), lambda i, p: (i, 0)),
        scratch_shapes=[pltpu.VMEM((2, 2, tc, d), F32), pltpu.SemaphoreType.DMA((2,))],
    )
    return pl.pallas_call(
        functools.partial(_combine_kernel, tc=tc),
        grid_spec=grid_spec,
        out_shape=jax.ShapeDtypeStruct((t, d), F32),
        compiler_params=_cp(("arbitrary",)),
        name="moe_combine",
    )(pos, x1, gate, y)


def _rope_lane_tables(seq, dim):
    rot = dim // ROPE_FRAC
    half = rot // 2
    inv = 1.0 / (ROPE_THETA ** (jnp.arange(0, rot, 2, dtype=F32) / rot))
    ang = jnp.arange(seq, dtype=F32)[:, None] * inv[None, :]
    cos, sin = jnp.cos(ang), jnp.sin(ang)
    z_half = jnp.zeros((seq, half), F32)
    z_rest = jnp.zeros((seq, dim - rot), F32)
    c = jnp.concatenate([cos, cos, jnp.ones((seq, dim - rot), F32)], axis=-1)
    sa = jnp.concatenate([-sin, z_half, z_rest], axis=-1)
    sb = jnp.concatenate([z_half, sin, z_rest], axis=-1)
    reps = LANES // dim
    return jnp.stack([jnp.tile(c, (1, reps)), jnp.tile(sa, (1, reps)), jnp.tile(sb, (1, reps))])


def _routing_plan(eid, n_tiles):
    tm = MOE_TILE
    e = eid[:, :2].reshape(-1)
    n_assign = e.shape[0]
    onehot = (e[:, None] == jnp.arange(N_EXPERTS, dtype=I32)[None, :]).astype(I32)
    csum = jnp.cumsum(onehot, axis=0)
    rank = jnp.take_along_axis(csum, e[:, None], axis=1)[:, 0] - 1
    counts = csum[-1]
    tiles_per = (counts + tm - 1) // tm
    tile_end = jnp.cumsum(tiles_per)
    tile_start = tile_end - tiles_per
    n_used = tile_end[-1]
    pos = tile_start[e] * tm + rank
    tile_ids = jnp.minimum(jnp.arange(n_tiles, dtype=I32), n_used - 1)
    tile_expert = jnp.sum((tile_end[None, :] <= tile_ids[:, None]).astype(I32), axis=1)
    row_token = jnp.zeros((n_tiles * tm,), I32).at[pos].set(jnp.arange(n_assign, dtype=I32) // 2)
    return tile_expert, n_used.reshape(1).astype(I32), row_token, pos.astype(I32)


def kernel(x, norm1_g, w_in, a_qn_g, a_kn_g, a_lambda, a_subln_g, b_qn_g, b_kn_g, c_qn_g, c_kn_g,
           idx_kn_g, w_out, norm2_g, w_group, b_group, w_expert, b_expert, w_gate, w_up, w_down):
    b, s, d = x.shape
    depth = w_in.shape[0]
    assert d == D_MODEL and s % MOBA_BLOCK == 0
    t = b * s
    n_tiles = (2 * t) // MOE_TILE + N_EXPERTS
    tab64 = _rope_lane_tables(s, A_QK)
    tab128 = _rope_lane_tables(s, HEAD_DIM)
    tile2 = lambda v: jnp.tile(v, 2).reshape(1, LANES)
    row = lambda v: v.reshape(1, LANES)

    w_main = w_in[:, :, :MAIN_COLS].astype(BF16)
    w_tail = jnp.pad(w_in[:, :, MAIN_COLS:], ((0, 0), (0, 0), (0, LANES - TAIL_COLS)))

    xf = x.reshape(t, d)
    for l in range(depth):
        lam_init = 0.8 - 0.6 * math.exp(-0.3 * l)
        h = _rmsnorm_bf16(xf, norm1_g[l])
        proj = _matmul(h, w_main, l, BF16, 1024, 1024, "in_proj")
        tail = _matmul(h, w_tail, l, F32, 1024, LANES, "in_proj_tail")

        oa = _attn_a(proj, tab64, tile2(a_qn_g[l]), tile2(a_kn_g[l]), a_lambda[l], row(a_subln_g[l]),
                     b=b, s=s, lam_init=lam_init)
        ob = _attn_b(proj, tab128, row(b_qn_g[l]), row(b_kn_g[l]), b=b, s=s)
        ikg = jnp.pad(idx_kn_g[l], (0, LANES - IDX_DIM)).reshape(1, LANES)
        oc = _attn_c(proj, tail, tab128, tab64, row(c_qn_g[l]), row(c_kn_g[l]), ikg, b=b, s=s)

        w_router = jnp.pad(jnp.concatenate([w_group[l], w_expert[l]], axis=1),
                           ((0, 0), (0, LANES - N_GROUPS - N_EXPERTS)))
        b_router = jnp.pad(jnp.concatenate([b_group[l], b_expert[l]]),
                           (0, LANES - N_GROUPS - N_EXPERTS)).reshape(1, LANES)
        x1, tn, eid, gate = _out_router(xf, oa, ob, oc, _cast_bf16(w_out, l, d), norm2_g[l], w_router, b_router)

        tile_expert, n_used, row_token, pos = _routing_plan(eid, n_tiles)
        y = _moe(tn, tile_expert, n_used, row_token, w_gate, w_up, w_down, layer=l, n_tiles=n_tiles)
        xf = _combine(pos, x1, gate, y)
    return xf.reshape(b, s, d)
```

```python
import functools
import math

import jax
import jax.numpy as jnp
from jax import lax
from jax.experimental import pallas as pl
from jax.experimental.pallas import tpu as pltpu

F32 = jnp.float32
BF16 = jnp.bfloat16
I32 = jnp.int32

D_MODEL = 2048
HEAD_DIM = 128
A_HEADS = 6
A_QK = 64
B_HEADS = 5
MOBA_BLOCK = 256
MOBA_TOPK = 3
C_HEADS = 5
IDX_HEADS = 16
IDX_DIM = 64
DSA_TOPK = 256
ROPE_THETA = 500000.0
ROPE_FRAC = 4
EPS = 1e-6
N_GROUPS = 4
EXPERTS_PER_GROUP = 8
N_EXPERTS = N_GROUPS * EXPERTS_PER_GROUP
EXPERT_FF = 512

LANES = 128
MAIN_COLS = 6144
TAIL_COLS = 80
QA_C, KA_C, VA_C = 0, 6, 12
QB_C, KB_C, VB_C = 18, 23, 28
QC_C, KC_C, VC_C = 33, 38, 39
QI_C = 40
NEG = -1e30
LOG2E = 1.4426950408889634
INT_MIN = -(2 ** 31)
MOE_TILE = 256
VMEM_LIMIT = 56 * 1024 * 1024

_NT = (((1,), (1,)), ((), ()))


def _cp(sem, vmem=VMEM_LIMIT):
    return pltpu.CompilerParams(dimension_semantics=sem, vmem_limit_bytes=vmem)


def _rms_full(x, g):
    ms = jnp.mean(x * x, axis=-1, keepdims=True)
    return x * lax.rsqrt(ms + EPS) * g


def _rms_halves(x, g):
    lane = lax.broadcasted_iota(I32, x.shape, 1)
    lo = lane < 64
    x2 = x * x
    s_lo = jnp.sum(jnp.where(lo, x2, 0.0), axis=-1, keepdims=True)
    s_hi = jnp.sum(jnp.where(lo, 0.0, x2), axis=-1, keepdims=True)
    r = jnp.where(lo, lax.rsqrt(s_lo * (1.0 / 64) + EPS), lax.rsqrt(s_hi * (1.0 / 64) + EPS))
    return x * r * g


def _rope(x, tab_ref, half):
    c = tab_ref[0]
    sa = tab_ref[1]
    sb = tab_ref[2]
    return x * c + pltpu.roll(x, LANES - half, 1) * sa + pltpu.roll(x, half, 1) * sb


def _ones_col(rows):
    lane = lax.broadcasted_iota(I32, (rows, LANES), 1)
    return jnp.where(lane == 0, 1.0, 0.0).astype(BF16)


def _lane_max2(mrun, s):
    return jnp.maximum(mrun, jnp.maximum(s[:, :LANES], s[:, LANES:]))


def _attend(qs, k_refs, va_refs, s_refs, qt, tk, *, diag_mask=None, mask_fn=None):
    n = len(qs)
    rows = qs[0].shape[0]
    static = isinstance(qt, int)

    def loop(lo, hi, body, init):
        if not static:
            return lax.fori_loop(lo, hi, body, init)
        for c in range(lo, hi):
            init = body(c, init)
        return init

    def chunk(ref, c):
        if static:
            return ref[c * tk:(c + 1) * tk, :]
        return ref[pl.ds(pl.multiple_of(c * tk, tk), tk), :]

    def scores(h, c):
        s = lax.dot_general(qs[h], chunk(k_refs[h], c), _NT, preferred_element_type=F32)
        return s if mask_fn is None else mask_fn(c, s)

    def first(c, mruns):
        out = []
        for h in range(n):
            s = scores(h, c)
            s_refs[h][c] = s
            out.append(_lane_max2(mruns[h], s))
        return tuple(out)

    init = tuple(jnp.full((rows, LANES), NEG, F32) for _ in range(n))
    if diag_mask is None:
        mruns = loop(0, qt + 1, first, init)
    else:
        mruns = list(loop(0, qt, first, init))
        for h in range(n):
            s = jnp.where(diag_mask, scores(h, qt), NEG)
            s_refs[h][qt] = s
            mruns[h] = _lane_max2(mruns[h], s)
    ms = [jnp.max(mr, axis=-1, keepdims=True) for mr in mruns]

    def second(c, accs):
        return tuple(
            accs[h] + jnp.dot(jnp.exp2(s_refs[h][c] - ms[h]).astype(BF16), chunk(va_refs[h], c),
                              preferred_element_type=F32)
            for h in range(n))

    accs = loop(0, qt + 1, second, tuple(jnp.zeros((rows, 2 * LANES), F32) for _ in range(n)))
    return [a[:, :LANES] / a[:, LANES:LANES + 1] for a in accs]


def _attend_unrolled(nq, qt, finish, qs, k_refs, va_refs, s_refs, tk, **kwargs):
    for j in range(nq):
        pl.when(qt == j)(lambda j=j: finish(_attend(qs, k_refs, va_refs, s_refs, j, tk, **kwargs)))


def _norm_kernel(x_ref, g_ref, o_ref):
    o_ref[...] = _rms_full(x_ref[...], g_ref[...]).astype(o_ref.dtype)


def _rmsnorm_bf16(x, g):
    t, d = x.shape
    tm = 512
    return pl.pallas_call(
        _norm_kernel,
        grid=(t // tm,),
        in_specs=[pl.BlockSpec((tm, d), lambda i: (i, 0)), pl.BlockSpec((1, d), lambda i: (0, 0))],
        out_specs=pl.BlockSpec((tm, d), lambda i: (i, 0)),
        out_shape=jax.ShapeDtypeStruct((t, d), BF16),
        compiler_params=_cp(("parallel",)),
        name="rmsnorm",
    )(x, g.reshape(1, d))


def _cast_kernel(x_ref, o_ref):
    o_ref[...] = x_ref[...].astype(o_ref.dtype)


def _cast_bf16(w, layer, ncols):
    r = w.shape[1]
    tr, tc = 256, min(ncols, 1024)
    return pl.pallas_call(
        _cast_kernel,
        grid=(r // tr, ncols // tc),
        in_specs=[pl.BlockSpec((None, tr, tc), lambda i, j: (layer, i, j))],
        out_specs=pl.BlockSpec((tr, tc), lambda i, j: (i, j)),
        out_shape=jax.ShapeDtypeStruct((r, ncols), BF16),
        compiler_params=_cp(("parallel", "parallel")),
        name="cast_bf16",
    )(w)


def _matmul_kernel(x_ref, w_ref, o_ref):
    w = w_ref[...].astype(x_ref.dtype)
    o_ref[...] = jnp.dot(x_ref[...], w, preferred_element_type=F32).astype(o_ref.dtype)


def _matmul(x, w, layer, out_dtype, tm, tn, name):
    m, k = x.shape
    n = w.shape[2]
    return pl.pallas_call(
        _matmul_kernel,
        grid=(n // tn, m // tm),
        in_specs=[pl.BlockSpec((tm, k), lambda j, i: (i, 0)),
                  pl.BlockSpec((None, k, tn), lambda j, i: (layer, 0, j))],
        out_specs=pl.BlockSpec((tm, tn), lambda j, i: (i, j)),
        out_shape=jax.ShapeDtypeStruct((m, n), out_dtype),
        compiler_params=_cp(("parallel", "parallel")),
        name=name,
    )(x, w)


def _attn_a_kernel(*refs, tq, nq, nh, lam_init):
    q_refs, k_refs, v_refs = refs[0:nh], refs[nh:2 * nh], refs[2 * nh:3 * nh]
    tabq_ref, tabk_ref, qg_ref, kg_ref, lam_ref, sg_ref, o_ref, ks_ref, va_ref, s_ref = refs[3 * nh:]
    qt = pl.program_id(2)

    @pl.when(qt == 0)
    def _():
        ones = _ones_col(ks_ref.shape[1])
        for h in range(nh):
            k = _rms_halves(k_refs[h][...].astype(F32), kg_ref[...])
            ks_ref[h] = _rope(k, tabk_ref, 8).astype(BF16)
            va_ref[h, :, :LANES] = v_refs[h][...]
            va_ref[h, :, LANES:] = ones

    lane = lax.broadcasted_iota(I32, (tq, LANES), 1)
    qs = []
    for h in range(nh):
        q = _rms_halves(q_refs[h][...].astype(F32), qg_ref[...])
        q = _rope(q, tabq_ref, 8) * (A_QK ** -0.5 * LOG2E)
        qs.append(jnp.concatenate([jnp.where(lane < 64, q, 0.0), jnp.where(lane < 64, 0.0, q)],
                                  axis=0).astype(BF16))

    lp = lam_ref[...]
    lam = (jnp.exp(jnp.sum(lp[0:1] * lp[1:2], axis=-1, keepdims=True))
           - jnp.exp(jnp.sum(lp[2:3] * lp[3:4], axis=-1, keepdims=True)) + lam_init)

    row = lax.broadcasted_iota(I32, (2 * tq, tq), 0)
    row = jnp.where(row >= tq, row - tq, row)
    causal = lax.broadcasted_iota(I32, (2 * tq, tq), 1) <= row
    def finish(outs):
        for h in range(nh):
            o = outs[h][:tq] - lam * outs[h][tq:]
            o = _rms_full(o, sg_ref[...]) * (1.0 - lam_init)
            o_ref[:, h * HEAD_DIM:(h + 1) * HEAD_DIM] = o.astype(o_ref.dtype)

    _attend_unrolled(nq, qt, finish, qs, [ks_ref.at[h] for h in range(nh)], [va_ref.at[h] for h in range(nh)],
                     [s_ref.at[h] for h in range(nh)], tq, diag_mask=causal)


def _attn_a(proj, tab64, qg, kg, lam_p, sg, *, b, s, lam_init):
    tq = 256
    nq = s // tq
    nh = 1
    ng = A_HEADS // nh
    kern = functools.partial(_attn_a_kernel, tq=tq, nq=nq, nh=nh, lam_init=lam_init)
    small = lambda shape: pl.BlockSpec(shape, lambda bi, g, qt: (0,) * len(shape))
    qspec = lambda h: pl.BlockSpec((tq, LANES), lambda bi, g, qt: (bi * nq + qt, QA_C + g * nh + h))
    kspec = lambda c0, h: pl.BlockSpec((s, LANES), lambda bi, g, qt: (bi, c0 + g * nh + h))
    return pl.pallas_call(
        kern,
        grid=(b, ng, nq),
        in_specs=([qspec(h) for h in range(nh)] + [kspec(KA_C, h) for h in range(nh)]
                  + [kspec(VA_C, h) for h in range(nh)]
                  + [pl.BlockSpec((3, tq, LANES), lambda bi, g, qt: (0, qt, 0)),
                     pl.BlockSpec((3, s, LANES), lambda bi, g, qt: (0, 0, 0)),
                     small((1, LANES)), small((1, LANES)), small((4, A_QK)), small((1, LANES))]),
        out_specs=pl.BlockSpec((tq, nh * HEAD_DIM), lambda bi, g, qt: (bi * nq + qt, g)),
        out_shape=jax.ShapeDtypeStruct((b * s, A_HEADS * HEAD_DIM), BF16),
        scratch_shapes=[pltpu.VMEM((nh, s, LANES), BF16), pltpu.VMEM((nh, s, 2 * LANES), BF16),
                        pltpu.VMEM((nh, nq, 2 * tq, tq), F32)],
        compiler_params=_cp(("parallel", "parallel", "arbitrary")),
        name="attn_diff",
    )(*([proj] * (3 * nh)), tab64, tab64, qg, kg, lam_p, sg)


def _attn_b_kernel(*refs, tq, nblk):
    nh = B_HEADS
    q_refs, k_refs, v_refs = refs[0:nh], refs[nh:2 * nh], refs[2 * nh:3 * nh]
    tabq_ref, tabk_ref, qg_ref, kg_ref, o_ref, ksa_ref, va_ref, kmean_ref, s_ref = refs[3 * nh:]
    qt = pl.program_id(1)
    s_len = k_refs[0].shape[0]

    @pl.when(qt == 0)
    def _():
        blk = lax.shift_right_logical(lax.broadcasted_iota(I32, (s_len, LANES), 0), MOBA_BLOCK.bit_length() - 1)
        onehot = jnp.where(lax.broadcasted_iota(I32, (s_len, LANES), 1) == blk, 1.0, 0.0).astype(BF16)
        ones = _ones_col(s_len)
        for h in range(nh):
            k = _rope(_rms_full(k_refs[h][...].astype(F32), kg_ref[...]), tabk_ref, 16)
            ksa_ref[h, :, :LANES] = k.astype(BF16)
            ksa_ref[h, :, LANES:] = onehot
            kmean_ref[h] = jnp.zeros((LANES, LANES), F32)
            kmean_ref[h, 0:nblk, :] = jnp.mean(k.reshape(nblk, MOBA_BLOCK, HEAD_DIM), axis=1)
            va_ref[h, :, :LANES] = v_refs[h][...]
            va_ref[h, :, LANES:] = ones

    nbp = -(-nblk // 8) * 8
    blk = lax.broadcasted_iota(I32, (nbp, tq), 0)
    q_aug = []
    for h in range(nh):
        q = _rope(_rms_full(q_refs[h][...].astype(F32), qg_ref[...]), tabq_ref, 16)
        gate = lax.dot_general(kmean_ref[h, 0:nbp, :], q, _NT, preferred_element_type=F32,
                               precision=lax.Precision.HIGHEST)
        g = jnp.where(blk < qt, gate, -jnp.inf)
        keep = blk == qt
        for _ in range(MOBA_TOPK):
            mx = jnp.max(g, axis=0, keepdims=True)
            first = jnp.min(jnp.where(g == mx, blk, nbp), axis=0, keepdims=True)
            pick = jnp.logical_and(blk == first, mx > -jnp.inf)
            keep = jnp.logical_or(keep, pick)
            g = jnp.where(pick, -jnp.inf, g)
        bias_t = jnp.concatenate([jnp.where(keep, 0.0, NEG), jnp.zeros((LANES - nbp, tq), F32)], axis=0)
        q_aug.append(jnp.concatenate([q * (HEAD_DIM ** -0.5 * LOG2E), bias_t.T], axis=1).astype(BF16))

    causal = lax.broadcasted_iota(I32, (tq, tq), 1) <= lax.broadcasted_iota(I32, (tq, tq), 0)
    def finish(outs):
        for h in range(nh):
            o_ref[:, h * HEAD_DIM:(h + 1) * HEAD_DIM] = outs[h].astype(o_ref.dtype)

    _attend_unrolled(nblk, qt, finish, q_aug, [ksa_ref.at[h] for h in range(nh)],
                     [va_ref.at[h] for h in range(nh)], [s_ref.at[h] for h in range(nh)], tq, diag_mask=causal)


def _attn_b(proj, tab128, qg, kg, *, b, s):
    tq = MOBA_BLOCK
    nq = s // tq
    nh = B_HEADS
    kern = functools.partial(_attn_b_kernel, tq=tq, nblk=nq)
    small = lambda shape: pl.BlockSpec(shape, lambda bi, qt: (0,) * len(shape))
    qspec = lambda h: pl.BlockSpec((tq, LANES), lambda bi, qt: (bi * nq + qt, QB_C + h))
    kspec = lambda c0, h: pl.BlockSpec((s, LANES), lambda bi, qt: (bi, c0 + h))
    return pl.pallas_call(
        kern,
        grid=(b, nq),
        in_specs=([qspec(h) for h in range(nh)] + [kspec(KB_C, h) for h in range(nh)]
                  + [kspec(VB_C, h) for h in range(nh)]
                  + [pl.BlockSpec((3, tq, LANES), lambda bi, qt: (0, qt, 0)),
                     pl.BlockSpec((3, s, LANES), lambda bi, qt: (0, 0, 0)),
                     small((1, LANES)), small((1, LANES))]),
        out_specs=pl.BlockSpec((tq, nh * HEAD_DIM), lambda bi, qt: (bi * nq + qt, 0)),
        out_shape=jax.ShapeDtypeStruct((b * s, nh * HEAD_DIM), BF16),
        scratch_shapes=[pltpu.VMEM((nh, s, 2 * LANES), BF16), pltpu.VMEM((nh, s, 2 * LANES), BF16),
                        pltpu.VMEM((nh, LANES, LANES), F32), pltpu.VMEM((nh, nq, tq, tq), F32)],
        compiler_params=_cp(("parallel", "arbitrary")),
        name="attn_moba",
    )(*([proj] * (3 * nh)), tab128, tab128, qg, kg)


def _attn_c_kernel(q0_ref, q1_ref, q2_ref, q3_ref, q4_ref, qi_ref, kc_ref, vc_ref, kt_ref, wt_ref,
                   t128q_ref, t128k_ref, t64q_ref, t64k_ref, cqg_ref, ckg_ref, ikg_ref,
                   o_ref, kcs_ref, kis_ref, va_ref, keys_ref, s_ref, *, tq, tk, nq, n_sel):
    qt = pl.program_id(1)

    @pl.when(qt == 0)
    def _():
        k = _rope(_rms_full(kc_ref[...].astype(F32), ckg_ref[...]), t128k_ref, 16)
        kcs_ref[...] = k.astype(BF16)
        va_ref[:, :LANES] = vc_ref[...]
        va_ref[:, LANES:] = _ones_col(vc_ref.shape[0])
        t = kt_ref[...]
        lane = lax.broadcasted_iota(I32, t.shape, 1)
        ms = jnp.sum(jnp.where(lane < IDX_DIM, t * t, 0.0), axis=-1, keepdims=True) * (1.0 / IDX_DIM)
        ki = _rope(t * lax.rsqrt(ms + EPS) * ikg_ref[...], t64k_ref, 8)
        kis_ref[...] = (ki + pltpu.roll(ki, 64, 1)).astype(BF16)

    lane = lax.broadcasted_iota(I32, (tq, LANES), 1)
    heads = []
    for j in range(IDX_HEADS // 2):
        x = _rope(qi_ref[:, j * LANES:(j + 1) * LANES].astype(F32), t64q_ref, 8)
        heads.append(jnp.where(lane < 64, x, 0.0).astype(BF16))
        heads.append(jnp.where(lane < 64, 0.0, x).astype(BF16))
    hg = 4
    qi_groups = [jnp.concatenate(heads[g:g + hg], axis=0) for g in range(0, IDX_HEADS, hg)]
    wt = wt_ref[...] * ((IDX_HEADS ** -0.5) * (IDX_DIM ** -0.5))
    w_cols = [wt[:, IDX_DIM + h:IDX_DIM + h + 1] for h in range(IDX_HEADS)]

    row = qt * tq + lax.broadcasted_iota(I32, (tq, tk), 0)
    col0 = lax.broadcasted_iota(I32, (tq, tk), 1)
    nch = (qt + 1) * (tq // tk)

    def index_keys(c, _):
        off = pl.multiple_of(c * tk, tk)
        kchunk = kis_ref[pl.ds(off, tk), :]
        sc = jnp.zeros((tq, tk), F32)
        for g, qg in enumerate(qi_groups):
            lg = lax.dot_general(qg, kchunk, _NT, preferred_element_type=F32)
            for i in range(hg):
                sc = sc + jnp.maximum(lg[i * tq:(i + 1) * tq], 0.0) * w_cols[g * hg + i]
        keys_ref[c] = jnp.where(col0 + c * tk <= row, sc, -jnp.inf)
        return 0

    lax.fori_loop(0, nch, index_keys, 0)

    def as_float(k):
        return pltpu.bitcast(jnp.where(k < 0, k ^ jnp.int32(0x7FFFFFFF), k), F32)

    def count_ge(cand):
        cf = as_float(cand)

        def body(c, acc):
            a = jnp.where(keys_ref[c] >= cf, 1.0, 0.0)
            return acc + (a[:, :LANES] + a[:, LANES:])
        tot = lax.fori_loop(0, nch, body, jnp.zeros((tq, LANES), F32))
        return jnp.sum(tot, axis=-1, keepdims=True)

    lo = jnp.where(count_ge(jnp.zeros((tq, 1), I32)) >= n_sel, jnp.int32(0), jnp.int32(INT_MIN))

    def bit_step(i, lo):
        cand = lo + lax.shift_left(jnp.int32(1), 30 - i)
        return jnp.where(count_ge(cand) >= n_sel, cand, lo)

    lo = lax.fori_loop(0, 31, bit_step, lo)
    thr = as_float(jnp.maximum(lo, jnp.int32(INT_MIN + 0x00800000)))

    qs = []
    for q_ref in (q0_ref, q1_ref, q2_ref, q3_ref, q4_ref):
        q = _rope(_rms_full(q_ref[...].astype(F32), cqg_ref[...]), t128q_ref, 16)
        qs.append((q * (HEAD_DIM ** -0.5 * LOG2E)).astype(BF16))
    q_all = jnp.concatenate(qs, axis=0)
    rows = C_HEADS * tq

    def select(c, s):
        return jnp.where((keys_ref[c] >= thr)[None], s.reshape(C_HEADS, tq, tk), NEG).reshape(rows, tk)

    o, = _attend([q_all], [kcs_ref], [va_ref], [s_ref], nch - 1, tk, mask_fn=select)
    for h in range(C_HEADS):
        o_ref[:, h * HEAD_DIM:(h + 1) * HEAD_DIM] = o[h * tq:(h + 1) * tq].astype(o_ref.dtype)


def _attn_c(proj, tail, tab128, tab64, cqg, ckg, ikg, *, b, s):
    tq, tk = 256, 256
    nq = s // tq
    n_sel = min(DSA_TOPK, s // 4)
    assert tq % tk == 0 and s % tq == 0
    kern = functools.partial(_attn_c_kernel, tq=tq, tk=tk, nq=nq, n_sel=n_sel)
    small = lambda shape: pl.BlockSpec(shape, lambda bi, qt: (0,) * len(shape))
    qspec = lambda h: pl.BlockSpec((tq, LANES), lambda bi, qt: (bi * nq + qt, QC_C + h))
    return pl.pallas_call(
        kern,
        grid=(b, nq),
        in_specs=[
            qspec(0), qspec(1), qspec(2), qspec(3), qspec(4),
            pl.BlockSpec((tq, IDX_HEADS * IDX_DIM), lambda bi, qt: (bi * nq + qt, QI_C // 8)),
            pl.BlockSpec((s, LANES), lambda bi, qt: (bi, KC_C)),
            pl.BlockSpec((s, LANES), lambda bi, qt: (bi, VC_C)),
            pl.BlockSpec((s, LANES), lambda bi, qt: (bi, 0)),
            pl.BlockSpec((tq, LANES), lambda bi, qt: (bi * nq + qt, 0)),
            pl.BlockSpec((3, tq, LANES), lambda bi, qt: (0, qt, 0)),
            pl.BlockSpec((3, s, LANES), lambda bi, qt: (0, 0, 0)),
            pl.BlockSpec((3, tq, LANES), lambda bi, qt: (0, qt, 0)),
            pl.BlockSpec((3, s, LANES), lambda bi, qt: (0, 0, 0)),
            small((1, LANES)), small((1, LANES)), small((1, LANES)),
        ],
        out_specs=pl.BlockSpec((tq, C_HEADS * HEAD_DIM), lambda bi, qt: (bi * nq + qt, 0)),
        out_shape=jax.ShapeDtypeStruct((b * s, C_HEADS * HEAD_DIM), BF16),
        scratch_shapes=[pltpu.VMEM((s, LANES), BF16), pltpu.VMEM((s, LANES), BF16),
                        pltpu.VMEM((s, 2 * LANES), BF16), pltpu.VMEM((s // tk, tq, tk), F32),
                        pltpu.VMEM((s // tk, C_HEADS * tq, tk), F32)],
        compiler_params=_cp(("parallel", "arbitrary")),
        name="attn_dsa",
    )(proj, proj, proj, proj, proj, proj, proj, proj, tail, tail,
      tab128, tab128, tab64, tab64, cqg, ckg, ikg)


def _out_router_kernel(x_ref, oa_ref, ob_ref, oc_ref, w_ref, g2_ref, wr_ref, br_ref,
                       x1_ref, t_ref, eid_ref, gate_ref):
    na = A_HEADS * HEAD_DIM
    nb = na + B_HEADS * HEAD_DIM
    x1 = (x_ref[...]
          + jnp.dot(oa_ref[...], w_ref[0:na, :], preferred_element_type=F32)
          + jnp.dot(ob_ref[...], w_ref[na:nb, :], preferred_element_type=F32)
          + jnp.dot(oc_ref[...], w_ref[nb:, :], preferred_element_type=F32))
    x1_ref[...] = x1
    t = _rms_full(x1, g2_ref[...])
    t_ref[...] = t

    lg = jnp.dot(t, wr_ref[...], preferred_element_type=F32, precision=lax.Precision.HIGHEST) + br_ref[...]
    lane = lax.broadcasted_iota(I32, lg.shape, 1)
    ninf = -jnp.inf
    gl = jnp.where(lane < N_GROUPS, lg, ninf)
    gm = jnp.max(gl, axis=-1, keepdims=True)
    ge = jnp.exp(gl - gm)
    g_prob = ge / jnp.sum(ge, axis=-1, keepdims=True)
    g_idx = jnp.min(jnp.where(gl == gm, lane, LANES), axis=-1, keepdims=True)
    g_w = jnp.sum(jnp.where(lane == g_idx, g_prob, 0.0), axis=-1, keepdims=True)

    e0 = N_GROUPS + g_idx * EXPERTS_PER_GROUP
    emask = jnp.logical_and(lane >= e0, lane < e0 + EXPERTS_PER_GROUP)
    el = jnp.where(emask, lg, ninf)
    em = jnp.max(el, axis=-1, keepdims=True)
    ee = jnp.exp(el - em)
    ep = jnp.where(emask, ee / jnp.sum(ee, axis=-1, keepdims=True), ninf)
    v1 = jnp.max(ep, axis=-1, keepdims=True)
    i1 = jnp.min(jnp.where(ep == v1, lane, LANES), axis=-1, keepdims=True)
    ep2 = jnp.where(lane == i1, ninf, ep)
    v2 = jnp.max(ep2, axis=-1, keepdims=True)
    i2 = jnp.min(jnp.where(ep2 == v2, lane, LANES), axis=-1, keepdims=True)
    den = v1 + v2
    eid_ref[...] = jnp.where(lane == 0, i1 - N_GROUPS, jnp.where(lane == 1, i2 - N_GROUPS, 0))
    gate_ref[...] = jnp.where(lane == 0, g_w * (v1 / den), jnp.where(lane == 1, g_w * (v2 / den), 0.0))


def _out_router(x, oa, ob, oc, w_out_bf, g2, w_router, b_router):
    t, d = x.shape
    tm = 256
    row = lambda c: pl.BlockSpec((tm, c), lambda i: (i, 0))
    full = lambda r, c: pl.BlockSpec((r, c), lambda i: (0, 0))
    return pl.pallas_call(
        _out_router_kernel,
        grid=(t // tm,),
        in_specs=[row(d), row(oa.shape[1]), row(ob.shape[1]), row(oc.shape[1]),
                  full(d, d), full(1, d), full(d, LANES), full(1, LANES)],
        out_specs=[row(d), row(d), row(LANES), row(LANES)],
        out_shape=[jax.ShapeDtypeStruct((t, d), F32), jax.ShapeDtypeStruct((t, d), F32),
                   jax.ShapeDtypeStruct((t, LANES), I32), jax.ShapeDtypeStruct((t, LANES), F32)],
        compiler_params=_cp(("parallel",)),
        name="out_proj_router",
    )(x, oa, ob, oc, w_out_bf, g2.reshape(1, d), w_router, b_router)


def _moe_kernel(texp_ref, nused_ref, rtok_ref, t_hbm, wg_ref, wu_ref, wd_ref,
                y_ref, xbuf, sem, wg_bf, wu_bf, wd_bf, *, tm):
    i = pl.program_id(0)
    nu = nused_ref[0]

    def gather(tile, slot):
        def body(r, _):
            tok = rtok_ref[tile * tm + r]
            pltpu.make_async_copy(t_hbm.at[pl.ds(tok, 1)], xbuf.at[slot, pl.ds(r, 1)], sem.at[slot]).start()
            return 0
        lax.fori_loop(0, tm, body, 0, unroll=8)

    @pl.when(i == 0)
    def _():
        gather(0, 0)

    @pl.when(i + 1 < nu)
    def _():
        gather(i + 1, (i + 1) % 2)

    @pl.when(i < nu)
    def _():
        slot = i % 2
        pltpu.make_async_copy(xbuf.at[slot], xbuf.at[slot], sem.at[slot]).wait()

        changed = jnp.logical_or(i == 0, texp_ref[i] != texp_ref[jnp.maximum(i - 1, 0)])

        @pl.when(changed)
        def _():
            wg_bf[...] = wg_ref[...].astype(BF16)
            wu_bf[...] = wu_ref[...].astype(BF16)
            wd_bf[...] = wd_ref[...].astype(BF16)

        x = xbuf[slot].astype(BF16)
        g = jnp.dot(x, wg_bf[...], preferred_element_type=F32)
        u = jnp.dot(x, wu_bf[...], preferred_element_type=F32)
        h = (g * jax.nn.sigmoid(g)) * u
        y_ref[...] = jnp.dot(h.astype(BF16), wd_bf[...], preferred_element_type=F32)

    @pl.when(i >= nu)
    def _():
        y_ref[...] = jnp.zeros_like(y_ref)


def _moe(t, tile_expert, n_used, row_token, w_gate, w_up, w_down, *, layer, n_tiles):
    tm = MOE_TILE
    d = t.shape[1]
    grid_spec = pltpu.PrefetchScalarGridSpec(
        num_scalar_prefetch=3,
        grid=(n_tiles,),
        in_specs=[
            pl.BlockSpec(memory_space=pl.ANY),
            pl.BlockSpec((None, None, d, EXPERT_FF), lambda i, te, nu, rt: (layer, te[i], 0, 0)),
            pl.BlockSpec((None, None, d, EXPERT_FF), lambda i, te, nu, rt: (layer, te[i], 0, 0)),
            pl.BlockSpec((None, None, EXPERT_FF, d), lambda i, te, nu, rt: (layer, te[i], 0, 0)),
        ],
        out_specs=pl.BlockSpec((tm, d), lambda i, te, nu, rt: (i, 0)),
        scratch_shapes=[pltpu.VMEM((2, tm, d), F32), pltpu.SemaphoreType.DMA((2,)),
                        pltpu.VMEM((d, EXPERT_FF), BF16), pltpu.VMEM((d, EXPERT_FF), BF16),
                        pltpu.VMEM((EXPERT_FF, d), BF16)],
    )
    return pl.pallas_call(
        functools.partial(_moe_kernel, tm=tm),
        grid_spec=grid_spec,
        out_shape=jax.ShapeDtypeStruct((n_tiles * tm, d), F32),
        compiler_params=_cp(("arbitrary",)),
        name="moe_experts",
    )(tile_expert, n_used, row_token, t, w_gate, w_up, w_down)


def _combine_kernel(pos_ref, x1_ref, gate_ref, gn_ref, y_hbm, o_ref, *rest, tc):
    h_ref = rest[0] if len(rest) == 3 else None
    buf, sem = rest[-2:]
    i = pl.program_id(0)
    n = pl.num_programs(0)

    def gather(tile, slot):
        def body(r, _):
            a = (tile * tc + r) * 2
            pltpu.make_async_copy(y_hbm.at[pl.ds(pos_ref[a], 1)], buf.at[slot, 0, pl.ds(r, 1)], sem.at[slot]).start()
            pltpu.make_async_copy(y_hbm.at[pl.ds(pos_ref[a + 1], 1)], buf.at[slot, 1, pl.ds(r, 1)], sem.at[slot]).start()
            return 0
        lax.fori_loop(0, tc, body, 0, unroll=8)

    @pl.when(i == 0)
    def _():
        gather(0, 0)

    @pl.when(i + 1 < n)
    def _():
        gather(i + 1, (i + 1) % 2)

    slot = i % 2
    pltpu.make_async_copy(buf.at[slot], buf.at[slot], sem.at[slot]).wait()
    gt = gate_ref[...]
    x2 = x1_ref[...] + gt[:, 0:1] * buf[slot, 0] + gt[:, 1:2] * buf[slot, 1]
    o_ref[...] = x2
    if h_ref is not None:
        h_ref[...] = _rms_full(x2, gn_ref[...]).astype(h_ref.dtype)


def _combine(pos, x1, gate, y, g_next, emit_h):
    t, d = x1.shape
    tc = 256
    row = pl.BlockSpec((tc, d), lambda i, p: (i, 0))
    grid_spec = pltpu.PrefetchScalarGridSpec(
        num_scalar_prefetch=1,
        grid=(t // tc,),
        in_specs=[row, pl.BlockSpec((tc, LANES), lambda i, p: (i, 0)), pl.BlockSpec((1, d), lambda i, p: (0, 0)),
                  pl.BlockSpec(memory_space=pl.ANY)],
        out_specs=[row, row] if emit_h else row,
        scratch_shapes=[pltpu.VMEM((2, 2, tc, d), F32), pltpu.SemaphoreType.DMA((2,))],
    )
    x2_shape = jax.ShapeDtypeStruct((t, d), F32)
    return pl.pallas_call(
        functools.partial(_combine_kernel, tc=tc),
        grid_spec=grid_spec,
        out_shape=[x2_shape, jax.ShapeDtypeStruct((t, d), BF16)] if emit_h else x2_shape,
        compiler_params=_cp(("arbitrary",)),
        name="moe_combine",
    )(pos, x1, gate, g_next.reshape(1, d), y)


def _rope_lane_tables(seq, dim):
    rot = dim // ROPE_FRAC
    half = rot // 2
    inv = 1.0 / (ROPE_THETA ** (jnp.arange(0, rot, 2, dtype=F32) / rot))
    ang = jnp.arange(seq, dtype=F32)[:, None] * inv[None, :]
    cos, sin = jnp.cos(ang), jnp.sin(ang)
    z_half = jnp.zeros((seq, half), F32)
    z_rest = jnp.zeros((seq, dim - rot), F32)
    c = jnp.concatenate([cos, cos, jnp.ones((seq, dim - rot), F32)], axis=-1)
    sa = jnp.concatenate([-sin, z_half, z_rest], axis=-1)
    sb = jnp.concatenate([z_half, sin, z_rest], axis=-1)
    reps = LANES // dim
    return jnp.stack([jnp.tile(c, (1, reps)), jnp.tile(sa, (1, reps)), jnp.tile(sb, (1, reps))])


def _routing_plan(eid, n_tiles):
    tm = MOE_TILE
    e = eid[:, :2].reshape(-1)
    n_assign = e.shape[0]
    onehot = (e[:, None] == jnp.arange(N_EXPERTS, dtype=I32)[None, :]).astype(I32)
    csum = jnp.cumsum(onehot, axis=0)
    rank = jnp.take_along_axis(csum, e[:, None], axis=1)[:, 0] - 1
    counts = csum[-1]
    tiles_per = (counts + tm - 1) // tm
    tile_end = jnp.cumsum(tiles_per)
    tile_start = tile_end - tiles_per
    n_used = tile_end[-1]
    pos = tile_start[e] * tm + rank
    tile_ids = jnp.minimum(jnp.arange(n_tiles, dtype=I32), n_used - 1)
    tile_expert = jnp.sum((tile_end[None, :] <= tile_ids[:, None]).astype(I32), axis=1)
    row_token = jnp.zeros((n_tiles * tm,), I32).at[pos].set(jnp.arange(n_assign, dtype=I32) // 2)
    return tile_expert, n_used.reshape(1).astype(I32), row_token, pos.astype(I32)


def kernel(x, norm1_g, w_in, a_qn_g, a_kn_g, a_lambda, a_subln_g, b_qn_g, b_kn_g, c_qn_g, c_kn_g,
           idx_kn_g, w_out, norm2_g, w_group, b_group, w_expert, b_expert, w_gate, w_up, w_down):
    b, s, d = x.shape
    depth = w_in.shape[0]
    assert d == D_MODEL and s % MOBA_BLOCK == 0
    t = b * s
    n_tiles = (2 * t) // MOE_TILE + N_EXPERTS
    tab64 = _rope_lane_tables(s, A_QK)
    tab128 = _rope_lane_tables(s, HEAD_DIM)
    tile2 = lambda v: jnp.tile(v, 2).reshape(1, LANES)
    row = lambda v: v.reshape(1, LANES)

    w_main = w_in[:, :, :MAIN_COLS].astype(BF16)
    w_tail = jnp.pad(w_in[:, :, MAIN_COLS:], ((0, 0), (0, 0), (0, LANES - TAIL_COLS)))

    xf = x.reshape(t, d)
    h = _rmsnorm_bf16(xf, norm1_g[0])
    for l in range(depth):
        lam_init = 0.8 - 0.6 * math.exp(-0.3 * l)
        proj = _matmul(h, w_main, l, BF16, 1024, 1024, "in_proj")
        tail = _matmul(h, w_tail, l, F32, 1024, LANES, "in_proj_tail")

        oa = _attn_a(proj, tab64, tile2(a_qn_g[l]), tile2(a_kn_g[l]), a_lambda[l], row(a_subln_g[l]),
                     b=b, s=s, lam_init=lam_init)
        ob = _attn_b(proj, tab128, row(b_qn_g[l]), row(b_kn_g[l]), b=b, s=s)
        ikg = jnp.pad(idx_kn_g[l], (0, LANES - IDX_DIM)).reshape(1, LANES)
        oc = _attn_c(proj, tail, tab128, tab64, row(c_qn_g[l]), row(c_kn_g[l]), ikg, b=b, s=s)

        w_router = jnp.pad(jnp.concatenate([w_group[l], w_expert[l]], axis=1),
                           ((0, 0), (0, LANES - N_GROUPS - N_EXPERTS)))
        b_router = jnp.pad(jnp.concatenate([b_group[l], b_expert[l]]),
                           (0, LANES - N_GROUPS - N_EXPERTS)).reshape(1, LANES)
        x1, tn, eid, gate = _out_router(xf, oa, ob, oc, _cast_bf16(w_out, l, d), norm2_g[l], w_router, b_router)

        tile_expert, n_used, row_token, pos = _routing_plan(eid, n_tiles)
        y = _moe(tn, tile_expert, n_used, row_token, w_gate, w_up, w_down, layer=l, n_tiles=n_tiles)
        if l + 1 < depth:
            xf, h = _combine(pos, x1, gate, y, norm1_g[l + 1], True)
        else:
            xf = _combine(pos, x1, gate, y, norm1_g[l], False)
    return xf.reshape(b, s, d)
```

```python
import functools
import math

import jax
import jax.numpy as jnp
from jax import lax
from jax.experimental import pallas as pl
from jax.experimental.pallas import tpu as pltpu

F32 = jnp.float32
BF16 = jnp.bfloat16
I32 = jnp.int32

D_MODEL = 2048
HEAD_DIM = 128
A_HEADS = 6
A_QK = 64
B_HEADS = 5
MOBA_BLOCK = 256
MOBA_TOPK = 3
C_HEADS = 5
IDX_HEADS = 16
IDX_DIM = 64
DSA_TOPK = 256
ROPE_THETA = 500000.0
ROPE_FRAC = 4
EPS = 1e-6
N_GROUPS = 4
EXPERTS_PER_GROUP = 8
N_EXPERTS = N_GROUPS * EXPERTS_PER_GROUP
EXPERT_FF = 512

LANES = 128
MAIN_COLS = 6144
TAIL_COLS = 80
QA_C, KA_C, VA_C = 0, 6, 12
QB_C, KB_C, VB_C = 18, 23, 28
QC_C, KC_C, VC_C = 33, 38, 39
QI_C = 40
NEG = -1e30
LOG2E = 1.4426950408889634
INT_MIN = -(2 ** 31)
MOE_TILE = 256
VMEM_LIMIT = 56 * 1024 * 1024

_NT = (((1,), (1,)), ((), ()))


def _cp(sem, vmem=VMEM_LIMIT):
    return pltpu.CompilerParams(dimension_semantics=sem, vmem_limit_bytes=vmem)


def _rms_full(x, g):
    ms = jnp.mean(x * x, axis=-1, keepdims=True)
    return x * lax.rsqrt(ms + EPS) * g


def _rms_halves(x, g):
    lane = lax.broadcasted_iota(I32, x.shape, 1)
    lo = lane < 64
    x2 = x * x
    s_lo = jnp.sum(jnp.where(lo, x2, 0.0), axis=-1, keepdims=True)
    s_hi = jnp.sum(jnp.where(lo, 0.0, x2), axis=-1, keepdims=True)
    r = jnp.where(lo, lax.rsqrt(s_lo * (1.0 / 64) + EPS), lax.rsqrt(s_hi * (1.0 / 64) + EPS))
    return x * r * g


def _rope(x, tab_ref, half):
    c = tab_ref[0]
    sa = tab_ref[1]
    sb = tab_ref[2]
    return x * c + pltpu.roll(x, LANES - half, 1) * sa + pltpu.roll(x, half, 1) * sb


def _ones_col(rows):
    lane = lax.broadcasted_iota(I32, (rows, LANES), 1)
    return jnp.where(lane == 0, 1.0, 0.0).astype(BF16)


def _lane_max2(mrun, s):
    return jnp.maximum(mrun, jnp.maximum(s[:, :LANES], s[:, LANES:]))


def _attend(qs, k_refs, va_refs, s_refs, qt, tk, *, diag_mask=None, mask_fn=None):
    n = len(qs)
    rows = qs[0].shape[0]
    static = isinstance(qt, int)

    if static:
        assert mask_fn is None
        w0 = qt * tk
        outs = []
        for h in range(n):
            sd = lax.dot_general(qs[h], k_refs[h][w0:w0 + tk, :], _NT, preferred_element_type=F32)
            parts = [sd if diag_mask is None else jnp.where(diag_mask, sd, NEG)]
            if qt > 0:
                parts.insert(0, lax.dot_general(qs[h], k_refs[h][0:w0, :], _NT, preferred_element_type=F32))
            m = functools.reduce(jnp.maximum, [jnp.max(p, axis=-1, keepdims=True) for p in parts])
            p = jnp.concatenate([jnp.exp2(p - m).astype(BF16) for p in parts], axis=1)
            acc = jnp.dot(p, va_refs[h][0:w0 + tk, :], preferred_element_type=F32)
            outs.append(acc[:, :LANES] / acc[:, LANES:LANES + 1])
        return outs

    def loop(lo, hi, body, init):
        return lax.fori_loop(lo, hi, body, init)

    def chunk(ref, c):
        return ref[pl.ds(pl.multiple_of(c * tk, tk), tk), :]

    def scores(h, c):
        s = lax.dot_general(qs[h], chunk(k_refs[h], c), _NT, preferred_element_type=F32)
        return s if mask_fn is None else mask_fn(c, s)

    def first(c, mruns):
        out = []
        for h in range(n):
            s = scores(h, c)
            s_refs[h][c] = s
            out.append(_lane_max2(mruns[h], s))
        return tuple(out)

    init = tuple(jnp.full((rows, LANES), NEG, F32) for _ in range(n))
    if diag_mask is None:
        mruns = loop(0, qt + 1, first, init)
    else:
        mruns = list(loop(0, qt, first, init))
        for h in range(n):
            s = jnp.where(diag_mask, scores(h, qt), NEG)
            s_refs[h][qt] = s
            mruns[h] = _lane_max2(mruns[h], s)
    ms = [jnp.max(mr, axis=-1, keepdims=True) for mr in mruns]

    def second(c, accs):
        return tuple(
            accs[h] + jnp.dot(jnp.exp2(s_refs[h][c] - ms[h]).astype(BF16), chunk(va_refs[h], c),
                              preferred_element_type=F32)
            for h in range(n))

    accs = loop(0, qt + 1, second, tuple(jnp.zeros((rows, 2 * LANES), F32) for _ in range(n)))
    return [a[:, :LANES] / a[:, LANES:LANES + 1] for a in accs]


def _attend_unrolled(nq, qt, finish, qs, k_refs, va_refs, s_refs, tk, **kwargs):
    for j in range(nq):
        pl.when(qt == j)(lambda j=j: finish(_attend(qs, k_refs, va_refs, s_refs, j, tk, **kwargs)))


def _norm_kernel(x_ref, g_ref, o_ref):
    o_ref[...] = _rms_full(x_ref[...], g_ref[...]).astype(o_ref.dtype)


def _rmsnorm_bf16(x, g):
    t, d = x.shape
    tm = 512
    return pl.pallas_call(
        _norm_kernel,
        grid=(t // tm,),
        in_specs=[pl.BlockSpec((tm, d), lambda i: (i, 0)), pl.BlockSpec((1, d), lambda i: (0, 0))],
        out_specs=pl.BlockSpec((tm, d), lambda i: (i, 0)),
        out_shape=jax.ShapeDtypeStruct((t, d), BF16),
        compiler_params=_cp(("parallel",)),
        name="rmsnorm",
    )(x, g.reshape(1, d))


def _cast_kernel(x_ref, o_ref):
    o_ref[...] = x_ref[...].astype(o_ref.dtype)


def _cast_bf16(w, layer, ncols):
    r = w.shape[1]
    tr, tc = 256, min(ncols, 1024)
    return pl.pallas_call(
        _cast_kernel,
        grid=(r // tr, ncols // tc),
        in_specs=[pl.BlockSpec((None, tr, tc), lambda i, j: (layer, i, j))],
        out_specs=pl.BlockSpec((tr, tc), lambda i, j: (i, j)),
        out_shape=jax.ShapeDtypeStruct((r, ncols), BF16),
        compiler_params=_cp(("parallel", "parallel")),
        name="cast_bf16",
    )(w)


def _matmul_kernel(x_ref, w_ref, o_ref):
    w = w_ref[...].astype(x_ref.dtype)
    o_ref[...] = jnp.dot(x_ref[...], w, preferred_element_type=F32).astype(o_ref.dtype)


def _matmul(x, w, layer, out_dtype, tm, tn, name):
    m, k = x.shape
    n = w.shape[2]
    return pl.pallas_call(
        _matmul_kernel,
        grid=(n // tn, m // tm),
        in_specs=[pl.BlockSpec((tm, k), lambda j, i: (i, 0)),
                  pl.BlockSpec((None, k, tn), lambda j, i: (layer, 0, j))],
        out_specs=pl.BlockSpec((tm, tn), lambda j, i: (i, j)),
        out_shape=jax.ShapeDtypeStruct((m, n), out_dtype),
        compiler_params=_cp(("parallel", "parallel")),
        name=name,
    )(x, w)


def _attn_a_kernel(*refs, tq, nq, nh, lam_init):
    q_refs, k_refs, v_refs = refs[0:nh], refs[nh:2 * nh], refs[2 * nh:3 * nh]
    tabq_ref, tabk_ref, qg_ref, kg_ref, lam_ref, sg_ref, o_ref, ks_ref, va_ref, s_ref = refs[3 * nh:]
    qt = pl.program_id(2)

    @pl.when(qt == 0)
    def _():
        ones = _ones_col(ks_ref.shape[1])
        for h in range(nh):
            k = _rms_halves(k_refs[h][...].astype(F32), kg_ref[...])
            ks_ref[h] = _rope(k, tabk_ref, 8).astype(BF16)
            va_ref[h, :, :LANES] = v_refs[h][...]
            va_ref[h, :, LANES:] = ones

    lane = lax.broadcasted_iota(I32, (tq, LANES), 1)
    qs = []
    for h in range(nh):
        q = _rms_halves(q_refs[h][...].astype(F32), qg_ref[...])
        q = _rope(q, tabq_ref, 8) * (A_QK ** -0.5 * LOG2E)
        qs.append(jnp.concatenate([jnp.where(lane < 64, q, 0.0), jnp.where(lane < 64, 0.0, q)],
                                  axis=0).astype(BF16))

    lp = lam_ref[...]
    lam = (jnp.exp(jnp.sum(lp[0:1] * lp[1:2], axis=-1, keepdims=True))
           - jnp.exp(jnp.sum(lp[2:3] * lp[3:4], axis=-1, keepdims=True)) + lam_init)

    row = lax.broadcasted_iota(I32, (2 * tq, tq), 0)
    row = jnp.where(row >= tq, row - tq, row)
    causal = lax.broadcasted_iota(I32, (2 * tq, tq), 1) <= row
    def finish(outs):
        for h in range(nh):
            o = outs[h][:tq] - lam * outs[h][tq:]
            o = _rms_full(o, sg_ref[...]) * (1.0 - lam_init)
            o_ref[:, h * HEAD_DIM:(h + 1) * HEAD_DIM] = o.astype(o_ref.dtype)

    _attend_unrolled(nq, qt, finish, qs, [ks_ref.at[h] for h in range(nh)], [va_ref.at[h] for h in range(nh)],
                     [s_ref.at[h] for h in range(nh)], tq, diag_mask=causal)


def _attn_a(proj, tab64, qg, kg, lam_p, sg, *, b, s, lam_init):
    tq = 256
    nq = s // tq
    nh = 2
    ng = A_HEADS // nh
    kern = functools.partial(_attn_a_kernel, tq=tq, nq=nq, nh=nh, lam_init=lam_init)
    small = lambda shape: pl.BlockSpec(shape, lambda bi, g, qt: (0,) * len(shape))
    qspec = lambda h: pl.BlockSpec((tq, LANES), lambda bi, g, qt: (bi * nq + qt, QA_C + g * nh + h))
    kspec = lambda c0, h: pl.BlockSpec((s, LANES), lambda bi, g, qt: (bi, c0 + g * nh + h))
    return pl.pallas_call(
        kern,
        grid=(b, ng, nq),
        in_specs=([qspec(h) for h in range(nh)] + [kspec(KA_C, h) for h in range(nh)]
                  + [kspec(VA_C, h) for h in range(nh)]
                  + [pl.BlockSpec((3, tq, LANES), lambda bi, g, qt: (0, qt, 0)),
                     pl.BlockSpec((3, s, LANES), lambda bi, g, qt: (0, 0, 0)),
                     small((1, LANES)), small((1, LANES)), small((4, A_QK)), small((1, LANES))]),
        out_specs=pl.BlockSpec((tq, nh * HEAD_DIM), lambda bi, g, qt: (bi * nq + qt, g)),
        out_shape=jax.ShapeDtypeStruct((b * s, A_HEADS * HEAD_DIM), BF16),
        scratch_shapes=[pltpu.VMEM((nh, s, LANES), BF16), pltpu.VMEM((nh, s, 2 * LANES), BF16),
                        pltpu.VMEM((nh, nq, 2 * tq, tq), F32)],
        compiler_params=_cp(("parallel", "parallel", "arbitrary")),
        name="attn_diff",
    )(*([proj] * (3 * nh)), tab64, tab64, qg, kg, lam_p, sg)


def _attn_b_kernel(*refs, tq, nblk):
    nh = B_HEADS
    q_refs, k_refs, v_refs = refs[0:nh], refs[nh:2 * nh], refs[2 * nh:3 * nh]
    tabq_ref, tabk_ref, qg_ref, kg_ref, o_ref, ksa_ref, va_ref, kmean_ref, s_ref = refs[3 * nh:]
    qt = pl.program_id(1)
    s_len = k_refs[0].shape[0]

    @pl.when(qt == 0)
    def _():
        blk = lax.shift_right_logical(lax.broadcasted_iota(I32, (s_len, LANES), 0), MOBA_BLOCK.bit_length() - 1)
        onehot = jnp.where(lax.broadcasted_iota(I32, (s_len, LANES), 1) == blk, 1.0, 0.0).astype(BF16)
        ones = _ones_col(s_len)
        for h in range(nh):
            k = _rope(_rms_full(k_refs[h][...].astype(F32), kg_ref[...]), tabk_ref, 16)
            ksa_ref[h, :, :LANES] = k.astype(BF16)
            ksa_ref[h, :, LANES:] = onehot
            kmean_ref[h] = jnp.zeros((LANES, LANES), F32)
            kmean_ref[h, 0:nblk, :] = jnp.mean(k.reshape(nblk, MOBA_BLOCK, HEAD_DIM), axis=1)
            va_ref[h, :, :LANES] = v_refs[h][...]
            va_ref[h, :, LANES:] = ones

    nbp = -(-nblk // 8) * 8
    blk = lax.broadcasted_iota(I32, (nbp, tq), 0)
    q_aug = []
    for h in range(nh):
        q = _rope(_rms_full(q_refs[h][...].astype(F32), qg_ref[...]), tabq_ref, 16)
        gate = lax.dot_general(kmean_ref[h, 0:nbp, :], q, _NT, preferred_element_type=F32,
                               precision=lax.Precision.HIGHEST)
        g = jnp.where(blk < qt, gate, -jnp.inf)
        keep = blk == qt
        for _ in range(MOBA_TOPK):
            mx = jnp.max(g, axis=0, keepdims=True)
            first = jnp.min(jnp.where(g == mx, blk, nbp), axis=0, keepdims=True)
            pick = jnp.logical_and(blk == first, mx > -jnp.inf)
            keep = jnp.logical_or(keep, pick)
            g = jnp.where(pick, -jnp.inf, g)
        bias_t = jnp.concatenate([jnp.where(keep, 0.0, NEG), jnp.zeros((LANES - nbp, tq), F32)], axis=0)
        q_aug.append(jnp.concatenate([q * (HEAD_DIM ** -0.5 * LOG2E), bias_t.T], axis=1).astype(BF16))

    causal = lax.broadcasted_iota(I32, (tq, tq), 1) <= lax.broadcasted_iota(I32, (tq, tq), 0)
    def finish(outs):
        for h in range(nh):
            o_ref[:, h * HEAD_DIM:(h + 1) * HEAD_DIM] = outs[h].astype(o_ref.dtype)

    _attend_unrolled(nblk, qt, finish, q_aug, [ksa_ref.at[h] for h in range(nh)],
                     [va_ref.at[h] for h in range(nh)], [s_ref.at[h] for h in range(nh)], tq, diag_mask=causal)


def _attn_b(proj, tab128, qg, kg, *, b, s):
    tq = MOBA_BLOCK
    nq = s // tq
    nh = B_HEADS
    kern = functools.partial(_attn_b_kernel, tq=tq, nblk=nq)
    small = lambda shape: pl.BlockSpec(shape, lambda bi, qt: (0,) * len(shape))
    qspec = lambda h: pl.BlockSpec((tq, LANES), lambda bi, qt: (bi * nq + qt, QB_C + h))
    kspec = lambda c0, h: pl.BlockSpec((s, LANES), lambda bi, qt: (bi, c0 + h))
    return pl.pallas_call(
        kern,
        grid=(b, nq),
        in_specs=([qspec(h) for h in range(nh)] + [kspec(KB_C, h) for h in range(nh)]
                  + [kspec(VB_C, h) for h in range(nh)]
                  + [pl.BlockSpec((3, tq, LANES), lambda bi, qt: (0, qt, 0)),
                     pl.BlockSpec((3, s, LANES), lambda bi, qt: (0, 0, 0)),
                     small((1, LANES)), small((1, LANES))]),
        out_specs=pl.BlockSpec((tq, nh * HEAD_DIM), lambda bi, qt: (bi * nq + qt, 0)),
        out_shape=jax.ShapeDtypeStruct((b * s, nh * HEAD_DIM), BF16),
        scratch_shapes=[pltpu.VMEM((nh, s, 2 * LANES), BF16), pltpu.VMEM((nh, s, 2 * LANES), BF16),
                        pltpu.VMEM((nh, LANES, LANES), F32), pltpu.VMEM((nh, nq, tq, tq), F32)],
        compiler_params=_cp(("parallel", "arbitrary")),
        name="attn_moba",
    )(*([proj] * (3 * nh)), tab128, tab128, qg, kg)


def _attn_c_kernel(q0_ref, q1_ref, q2_ref, q3_ref, q4_ref, qi_ref, kc_ref, vc_ref, kt_ref, wt_ref,
                   t128q_ref, t128k_ref, t64q_ref, t64k_ref, cqg_ref, ckg_ref, ikg_ref,
                   o_ref, kcs_ref, kis_ref, va_ref, keys_ref, s_ref, *, tq, tk, nq, n_sel):
    qt = pl.program_id(1)

    @pl.when(qt == 0)
    def _():
        k = _rope(_rms_full(kc_ref[...].astype(F32), ckg_ref[...]), t128k_ref, 16)
        kcs_ref[...] = k.astype(BF16)
        va_ref[:, :LANES] = vc_ref[...]
        va_ref[:, LANES:] = _ones_col(vc_ref.shape[0])
        t = kt_ref[...]
        lane = lax.broadcasted_iota(I32, t.shape, 1)
        ms = jnp.sum(jnp.where(lane < IDX_DIM, t * t, 0.0), axis=-1, keepdims=True) * (1.0 / IDX_DIM)
        ki = _rope(t * lax.rsqrt(ms + EPS) * ikg_ref[...], t64k_ref, 8)
        kis_ref[...] = (ki + pltpu.roll(ki, 64, 1)).astype(BF16)

    lane = lax.broadcasted_iota(I32, (tq, LANES), 1)
    heads = []
    for j in range(IDX_HEADS // 2):
        x = _rope(qi_ref[:, j * LANES:(j + 1) * LANES].astype(F32), t64q_ref, 8)
        heads.append(jnp.where(lane < 64, x, 0.0).astype(BF16))
        heads.append(jnp.where(lane < 64, 0.0, x).astype(BF16))
    hg = 4
    qi_groups = [jnp.concatenate(heads[g:g + hg], axis=0) for g in range(0, IDX_HEADS, hg)]
    wt = wt_ref[...] * ((IDX_HEADS ** -0.5) * (IDX_DIM ** -0.5))
    w_cols = [wt[:, IDX_DIM + h:IDX_DIM + h + 1] for h in range(IDX_HEADS)]

    row = qt * tq + lax.broadcasted_iota(I32, (tq, tk), 0)
    col0 = lax.broadcasted_iota(I32, (tq, tk), 1)
    nch = (qt + 1) * (tq // tk)

    def index_keys(c, _):
        off = pl.multiple_of(c * tk, tk)
        kchunk = kis_ref[pl.ds(off, tk), :]
        sc = jnp.zeros((tq, tk), F32)
        for g, qg in enumerate(qi_groups):
            lg = lax.dot_general(qg, kchunk, _NT, preferred_element_type=F32)
            for i in range(hg):
                sc = sc + jnp.maximum(lg[i * tq:(i + 1) * tq], 0.0) * w_cols[g * hg + i]
        keys_ref[c] = jnp.where(col0 + c * tk <= row, sc, -jnp.inf)
        return 0

    lax.fori_loop(0, nch, index_keys, 0)

    def as_float(k):
        return pltpu.bitcast(jnp.where(k < 0, k ^ jnp.int32(0x7FFFFFFF), k), F32)

    def count_ge(cand):
        cf = as_float(cand)

        def body(c, acc):
            a = jnp.where(keys_ref[c] >= cf, 1.0, 0.0)
            return acc + (a[:, :LANES] + a[:, LANES:])
        tot = lax.fori_loop(0, nch, body, jnp.zeros((tq, LANES), F32))
        return jnp.sum(tot, axis=-1, keepdims=True)

    lo = jnp.where(count_ge(jnp.zeros((tq, 1), I32)) >= n_sel, jnp.int32(0), jnp.int32(INT_MIN))

    def bit_step(i, lo):
        cand = lo + lax.shift_left(jnp.int32(1), 30 - i)
        return jnp.where(count_ge(cand) >= n_sel, cand, lo)

    lo = lax.fori_loop(0, 31, bit_step, lo)
    thr = as_float(jnp.maximum(lo, jnp.int32(INT_MIN + 0x00800000)))

    qs = []
    for q_ref in (q0_ref, q1_ref, q2_ref, q3_ref, q4_ref):
        q = _rope(_rms_full(q_ref[...].astype(F32), cqg_ref[...]), t128q_ref, 16)
        qs.append((q * (HEAD_DIM ** -0.5 * LOG2E)).astype(BF16))
    q_all = jnp.concatenate(qs, axis=0)
    rows = C_HEADS * tq

    def select(c, s):
        return jnp.where((keys_ref[c] >= thr)[None], s.reshape(C_HEADS, tq, tk), NEG).reshape(rows, tk)

    o, = _attend([q_all], [kcs_ref], [va_ref], [s_ref], nch - 1, tk, mask_fn=select)
    for h in range(C_HEADS):
        o_ref[:, h * HEAD_DIM:(h + 1) * HEAD_DIM] = o[h * tq:(h + 1) * tq].astype(o_ref.dtype)


def _attn_c(proj, tail, tab128, tab64, cqg, ckg, ikg, *, b, s):
    tq, tk = 256, 256
    nq = s // tq
    n_sel = min(DSA_TOPK, s // 4)
    assert tq % tk == 0 and s % tq == 0
    kern = functools.partial(_attn_c_kernel, tq=tq, tk=tk, nq=nq, n_sel=n_sel)
    small = lambda shape: pl.BlockSpec(shape, lambda bi, qt: (0,) * len(shape))
    qspec = lambda h: pl.BlockSpec((tq, LANES), lambda bi, qt: (bi * nq + qt, QC_C + h))
    return pl.pallas_call(
        kern,
        grid=(b, nq),
        in_specs=[
            qspec(0), qspec(1), qspec(2), qspec(3), qspec(4),
            pl.BlockSpec((tq, IDX_HEADS * IDX_DIM), lambda bi, qt: (bi * nq + qt, QI_C // 8)),
            pl.BlockSpec((s, LANES), lambda bi, qt: (bi, KC_C)),
            pl.BlockSpec((s, LANES), lambda bi, qt: (bi, VC_C)),
            pl.BlockSpec((s, LANES), lambda bi, qt: (bi, 0)),
            pl.BlockSpec((tq, LANES), lambda bi, qt: (bi * nq + qt, 0)),
            pl.BlockSpec((3, tq, LANES), lambda bi, qt: (0, qt, 0)),
            pl.BlockSpec((3, s, LANES), lambda bi, qt: (0, 0, 0)),
            pl.BlockSpec((3, tq, LANES), lambda bi, qt: (0, qt, 0)),
            pl.BlockSpec((3, s, LANES), lambda bi, qt: (0, 0, 0)),
            small((1, LANES)), small((1, LANES)), small((1, LANES)),
        ],
        out_specs=pl.BlockSpec((tq, C_HEADS * HEAD_DIM), lambda bi, qt: (bi * nq + qt, 0)),
        out_shape=jax.ShapeDtypeStruct((b * s, C_HEADS * HEAD_DIM), BF16),
        scratch_shapes=[pltpu.VMEM((s, LANES), BF16), pltpu.VMEM((s, LANES), BF16),
                        pltpu.VMEM((s, 2 * LANES), BF16), pltpu.VMEM((s // tk, tq, tk), F32),
                        pltpu.VMEM((s // tk, C_HEADS * tq, tk), F32)],
        compiler_params=_cp(("parallel", "arbitrary")),
        name="attn_dsa",
    )(proj, proj, proj, proj, proj, proj, proj, proj, tail, tail,
      tab128, tab128, tab64, tab64, cqg, ckg, ikg)


def _out_router_kernel(x_ref, oa_ref, ob_ref, oc_ref, w_ref, g2_ref, wr_ref, br_ref,
                       x1_ref, t_ref, eid_ref, gate_ref):
    na = A_HEADS * HEAD_DIM
    nb = na + B_HEADS * HEAD_DIM
    x1 = (x_ref[...]
          + jnp.dot(oa_ref[...], w_ref[0:na, :], preferred_element_type=F32)
          + jnp.dot(ob_ref[...], w_ref[na:nb, :], preferred_element_type=F32)
          + jnp.dot(oc_ref[...], w_ref[nb:, :], preferred_element_type=F32))
    x1_ref[...] = x1
    t = _rms_full(x1, g2_ref[...])
    t_ref[...] = t

    lg = jnp.dot(t, wr_ref[...], preferred_element_type=F32, precision=lax.Precision.HIGHEST) + br_ref[...]
    lane = lax.broadcasted_iota(I32, lg.shape, 1)
    ninf = -jnp.inf
    gl = jnp.where(lane < N_GROUPS, lg, ninf)
    gm = jnp.max(gl, axis=-1, keepdims=True)
    ge = jnp.exp(gl - gm)
    g_prob = ge / jnp.sum(ge, axis=-1, keepdims=True)
    g_idx = jnp.min(jnp.where(gl == gm, lane, LANES), axis=-1, keepdims=True)
    g_w = jnp.sum(jnp.where(lane == g_idx, g_prob, 0.0), axis=-1, keepdims=True)

    e0 = N_GROUPS + g_idx * EXPERTS_PER_GROUP
    emask = jnp.logical_and(lane >= e0, lane < e0 + EXPERTS_PER_GROUP)
    el = jnp.where(emask, lg, ninf)
    em = jnp.max(el, axis=-1, keepdims=True)
    ee = jnp.exp(el - em)
    ep = jnp.where(emask, ee / jnp.sum(ee, axis=-1, keepdims=True), ninf)
    v1 = jnp.max(ep, axis=-1, keepdims=True)
    i1 = jnp.min(jnp.where(ep == v1, lane, LANES), axis=-1, keepdims=True)
    ep2 = jnp.where(lane == i1, ninf, ep)
    v2 = jnp.max(ep2, axis=-1, keepdims=True)
    i2 = jnp.min(jnp.where(ep2 == v2, lane, LANES), axis=-1, keepdims=True)
    den = v1 + v2
    eid_ref[...] = jnp.where(lane == 0, i1 - N_GROUPS, jnp.where(lane == 1, i2 - N_GROUPS, 0))
    gate_ref[...] = jnp.where(lane == 0, g_w * (v1 / den), jnp.where(lane == 1, g_w * (v2 / den), 0.0))


def _out_router(x, oa, ob, oc, w_out_bf, g2, w_router, b_router):
    t, d = x.shape
    tm = 256
    row = lambda c: pl.BlockSpec((tm, c), lambda i: (i, 0))
    full = lambda r, c: pl.BlockSpec((r, c), lambda i: (0, 0))
    return pl.pallas_call(
        _out_router_kernel,
        grid=(t // tm,),
        in_specs=[row(d), row(oa.shape[1]), row(ob.shape[1]), row(oc.shape[1]),
                  full(d, d), full(1, d), full(d, LANES), full(1, LANES)],
        out_specs=[row(d), row(d), row(LANES), row(LANES)],
        out_shape=[jax.ShapeDtypeStruct((t, d), F32), jax.ShapeDtypeStruct((t, d), F32),
                   jax.ShapeDtypeStruct((t, LANES), I32), jax.ShapeDtypeStruct((t, LANES), F32)],
        compiler_params=_cp(("parallel",)),
        name="out_proj_router",
    )(x, oa, ob, oc, w_out_bf, g2.reshape(1, d), w_router, b_router)


def _moe_kernel(texp_ref, nused_ref, rtok_ref, t_hbm, wg_ref, wu_ref, wd_ref,
                y_ref, xbuf, sem, wg_bf, wu_bf, wd_bf, *, tm):
    i = pl.program_id(0)
    nu = nused_ref[0]

    def gather(tile, slot):
        def body(r, _):
            tok = rtok_ref[tile * tm + r]
            pltpu.make_async_copy(t_hbm.at[pl.ds(tok, 1)], xbuf.at[slot, pl.ds(r, 1)], sem.at[slot]).start()
            return 0
        lax.fori_loop(0, tm, body, 0, unroll=8)

    @pl.when(i == 0)
    def _():
        gather(0, 0)

    @pl.when(i + 1 < nu)
    def _():
        gather(i + 1, (i + 1) % 2)

    @pl.when(i < nu)
    def _():
        slot = i % 2
        pltpu.make_async_copy(xbuf.at[slot], xbuf.at[slot], sem.at[slot]).wait()

        changed = jnp.logical_or(i == 0, texp_ref[i] != texp_ref[jnp.maximum(i - 1, 0)])

        @pl.when(changed)
        def _():
            wg_bf[...] = wg_ref[...].astype(BF16)
            wu_bf[...] = wu_ref[...].astype(BF16)
            wd_bf[...] = wd_ref[...].astype(BF16)

        x = xbuf[slot].astype(BF16)
        g = jnp.dot(x, wg_bf[...], preferred_element_type=F32)
        u = jnp.dot(x, wu_bf[...], preferred_element_type=F32)
        h = (g * jax.nn.sigmoid(g)) * u
        y_ref[...] = jnp.dot(h.astype(BF16), wd_bf[...], preferred_element_type=F32)

    @pl.when(i >= nu)
    def _():
        y_ref[...] = jnp.zeros_like(y_ref)


def _moe(t, tile_expert, n_used, row_token, w_gate, w_up, w_down, *, layer, n_tiles):
    tm = MOE_TILE
    d = t.shape[1]
    grid_spec = pltpu.PrefetchScalarGridSpec(
        num_scalar_prefetch=3,
        grid=(n_tiles,),
        in_specs=[
            pl.BlockSpec(memory_space=pl.ANY),
            pl.BlockSpec((None, None, d, EXPERT_FF), lambda i, te, nu, rt: (layer, te[i], 0, 0)),
            pl.BlockSpec((None, None, d, EXPERT_FF), lambda i, te, nu, rt: (layer, te[i], 0, 0)),
            pl.BlockSpec((None, None, EXPERT_FF, d), lambda i, te, nu, rt: (layer, te[i], 0, 0)),
        ],
        out_specs=pl.BlockSpec((tm, d), lambda i, te, nu, rt: (i, 0)),
        scratch_shapes=[pltpu.VMEM((2, tm, d), F32), pltpu.SemaphoreType.DMA((2,)),
                        pltpu.VMEM((d, EXPERT_FF), BF16), pltpu.VMEM((d, EXPERT_FF), BF16),
                        pltpu.VMEM((EXPERT_FF, d), BF16)],
    )
    return pl.pallas_call(
        functools.partial(_moe_kernel, tm=tm),
        grid_spec=grid_spec,
        out_shape=jax.ShapeDtypeStruct((n_tiles * tm, d), F32),
        compiler_params=_cp(("arbitrary",)),
        name="moe_experts",
    )(tile_expert, n_used, row_token, t, w_gate, w_up, w_down)


def _combine_kernel(pos_ref, x1_ref, gate_ref, gn_ref, y_hbm, o_ref, *rest, tc):
    h_ref = rest[0] if len(rest) == 3 else None
    buf, sem = rest[-2:]
    i = pl.program_id(0)
    n = pl.num_programs(0)

    def gather(tile, slot):
        def body(r, _):
            a = (tile * tc + r) * 2
            pltpu.make_async_copy(y_hbm.at[pl.ds(pos_ref[a], 1)], buf.at[slot, 0, pl.ds(r, 1)], sem.at[slot]).start()
            pltpu.make_async_copy(y_hbm.at[pl.ds(pos_ref[a + 1], 1)], buf.at[slot, 1, pl.ds(r, 1)], sem.at[slot]).start()
            return 0
        lax.fori_loop(0, tc, body, 0, unroll=8)

    @pl.when(i == 0)
    def _():
        gather(0, 0)

    @pl.when(i + 1 < n)
    def _():
        gather(i + 1, (i + 1) % 2)

    slot = i % 2
    pltpu.make_async_copy(buf.at[slot], buf.at[slot], sem.at[slot]).wait()
    gt = gate_ref[...]
    x2 = x1_ref[...] + gt[:, 0:1] * buf[slot, 0] + gt[:, 1:2] * buf[slot, 1]
    o_ref[...] = x2
    if h_ref is not None:
        h_ref[...] = _rms_full(x2, gn_ref[...]).astype(h_ref.dtype)


def _combine(pos, x1, gate, y, g_next, emit_h):
    t, d = x1.shape
    tc = 256
    row = pl.BlockSpec((tc, d), lambda i, p: (i, 0))
    grid_spec = pltpu.PrefetchScalarGridSpec(
        num_scalar_prefetch=1,
        grid=(t // tc,),
        in_specs=[row, pl.BlockSpec((tc, LANES), lambda i, p: (i, 0)), pl.BlockSpec((1, d), lambda i, p: (0, 0)),
                  pl.BlockSpec(memory_space=pl.ANY)],
        out_specs=[row, row] if emit_h else row,
        scratch_shapes=[pltpu.VMEM((2, 2, tc, d), F32), pltpu.SemaphoreType.DMA((2,))],
    )
    x2_shape = jax.ShapeDtypeStruct((t, d), F32)
    return pl.pallas_call(
        functools.partial(_combine_kernel, tc=tc),
        grid_spec=grid_spec,
        out_shape=[x2_shape, jax.ShapeDtypeStruct((t, d), BF16)] if emit_h else x2_shape,
        compiler_params=_cp(("arbitrary",)),
        name="moe_combine",
    )(pos, x1, gate, g_next.reshape(1, d), y)


def _rope_lane_tables(seq, dim):
    rot = dim // ROPE_FRAC
    half = rot // 2
    inv = 1.0 / (ROPE_THETA ** (jnp.arange(0, rot, 2, dtype=F32) / rot))
    ang = jnp.arange(seq, dtype=F32)[:, None] * inv[None, :]
    cos, sin = jnp.cos(ang), jnp.sin(ang)
    z_half = jnp.zeros((seq, half), F32)
    z_rest = jnp.zeros((seq, dim - rot), F32)
    c = jnp.concatenate([cos, cos, jnp.ones((seq, dim - rot), F32)], axis=-1)
    sa = jnp.concatenate([-sin, z_half, z_rest], axis=-1)
    sb = jnp.concatenate([z_half, sin, z_rest], axis=-1)
    reps = LANES // dim
    return jnp.stack([jnp.tile(c, (1, reps)), jnp.tile(sa, (1, reps)), jnp.tile(sb, (1, reps))])


def _routing_plan(eid, n_tiles):
    tm = MOE_TILE
    e = eid[:, :2].reshape(-1)
    n_assign = e.shape[0]
    onehot = (e[:, None] == jnp.arange(N_EXPERTS, dtype=I32)[None, :]).astype(I32)
    csum = jnp.cumsum(onehot, axis=0)
    rank = jnp.take_along_axis(csum, e[:, None], axis=1)[:, 0] - 1
    counts = csum[-1]
    tiles_per = (counts + tm - 1) // tm
    tile_end = jnp.cumsum(tiles_per)
    tile_start = tile_end - tiles_per
    n_used = tile_end[-1]
    pos = tile_start[e] * tm + rank
    tile_ids = jnp.minimum(jnp.arange(n_tiles, dtype=I32), n_used - 1)
    tile_expert = jnp.sum((tile_end[None, :] <= tile_ids[:, None]).astype(I32), axis=1)
    row_token = jnp.zeros((n_tiles * tm,), I32).at[pos].set(jnp.arange(n_assign, dtype=I32) // 2)
    return tile_expert, n_used.reshape(1).astype(I32), row_token, pos.astype(I32)


def kernel(x, norm1_g, w_in, a_qn_g, a_kn_g, a_lambda, a_subln_g, b_qn_g, b_kn_g, c_qn_g, c_kn_g,
           idx_kn_g, w_out, norm2_g, w_group, b_group, w_expert, b_expert, w_gate, w_up, w_down):
    b, s, d = x.shape
    depth = w_in.shape[0]
    assert d == D_MODEL and s % MOBA_BLOCK == 0
    t = b * s
    n_tiles = (2 * t) // MOE_TILE + N_EXPERTS
    tab64 = _rope_lane_tables(s, A_QK)
    tab128 = _rope_lane_tables(s, HEAD_DIM)
    tile2 = lambda v: jnp.tile(v, 2).reshape(1, LANES)
    row = lambda v: v.reshape(1, LANES)

    w_main = w_in[:, :, :MAIN_COLS].astype(BF16)
    w_tail = jnp.pad(w_in[:, :, MAIN_COLS:], ((0, 0), (0, 0), (0, LANES - TAIL_COLS)))

    xf = x.reshape(t, d)
    h = _rmsnorm_bf16(xf, norm1_g[0])
    for l in range(depth):
        lam_init = 0.8 - 0.6 * math.exp(-0.3 * l)
        proj = _matmul(h, w_main, l, BF16, 1024, 1024, "in_proj")
        tail = _matmul(h, w_tail, l, F32, 1024, LANES, "in_proj_tail")

        oa = _attn_a(proj, tab64, tile2(a_qn_g[l]), tile2(a_kn_g[l]), a_lambda[l], row(a_subln_g[l]),
                     b=b, s=s, lam_init=lam_init)
        ob = _attn_b(proj, tab128, row(b_qn_g[l]), row(b_kn_g[l]), b=b, s=s)
        ikg = jnp.pad(idx_kn_g[l], (0, LANES - IDX_DIM)).reshape(1, LANES)
        oc = _attn_c(proj, tail, tab128, tab64, row(c_qn_g[l]), row(c_kn_g[l]), ikg, b=b, s=s)

        w_router = jnp.pad(jnp.concatenate([w_group[l], w_expert[l]], axis=1),
                           ((0, 0), (0, LANES - N_GROUPS - N_EXPERTS)))
        b_router = jnp.pad(jnp.concatenate([b_group[l], b_expert[l]]),
                           (0, LANES - N_GROUPS - N_EXPERTS)).reshape(1, LANES)
        x1, tn, eid, gate = _out_router(xf, oa, ob, oc, _cast_bf16(w_out, l, d), norm2_g[l], w_router, b_router)

        tile_expert, n_used, row_token, pos = _routing_plan(eid, n_tiles)
        y = _moe(tn, tile_expert, n_used, row_token, w_gate, w_up, w_down, layer=l, n_tiles=n_tiles)
        if l + 1 < depth:
            xf, h = _combine(pos, x1, gate, y, norm1_g[l + 1], True)
        else:
            xf = _combine(pos, x1, gate, y, norm1_g[l], False)
    return xf.reshape(b, s, d)
```

```python
import functools
import math

import jax
import jax.numpy as jnp
from jax import lax
from jax.experimental import pallas as pl
from jax.experimental.pallas import tpu as pltpu

F32 = jnp.float32
BF16 = jnp.bfloat16
I32 = jnp.int32

D_MODEL = 2048
HEAD_DIM = 128
A_HEADS = 6
A_QK = 64
B_HEADS = 5
MOBA_BLOCK = 256
MOBA_TOPK = 3
C_HEADS = 5
IDX_HEADS = 16
IDX_DIM = 64
DSA_TOPK = 256
ROPE_THETA = 500000.0
ROPE_FRAC = 4
EPS = 1e-6
N_GROUPS = 4
EXPERTS_PER_GROUP = 8
N_EXPERTS = N_GROUPS * EXPERTS_PER_GROUP
EXPERT_FF = 512

LANES = 128
MAIN_COLS = 6144
TAIL_COLS = 80
QA_C, KA_C, VA_C = 0, 6, 12
QB_C, KB_C, VB_C = 18, 23, 28
QC_C, KC_C, VC_C = 33, 38, 39
QI_C = 40
NEG = -1e30
LOG2E = 1.4426950408889634
INT_MIN = -(2 ** 31)
MOE_TILE = 256
VMEM_LIMIT = 56 * 1024 * 1024

_NT = (((1,), (1,)), ((), ()))


def _cp(sem, vmem=VMEM_LIMIT):
    return pltpu.CompilerParams(dimension_semantics=sem, vmem_limit_bytes=vmem)


def _rms_full(x, g):
    ms = jnp.mean(x * x, axis=-1, keepdims=True)
    return x * lax.rsqrt(ms + EPS) * g


def _rms_halves(x, g):
    lane = lax.broadcasted_iota(I32, x.shape, 1)
    lo = lane < 64
    x2 = x * x
    s_lo = jnp.sum(jnp.where(lo, x2, 0.0), axis=-1, keepdims=True)
    s_hi = jnp.sum(jnp.where(lo, 0.0, x2), axis=-1, keepdims=True)
    r = jnp.where(lo, lax.rsqrt(s_lo * (1.0 / 64) + EPS), lax.rsqrt(s_hi * (1.0 / 64) + EPS))
    return x * r * g


def _rope(x, tab_ref, half):
    c = tab_ref[0]
    sa = tab_ref[1]
    sb = tab_ref[2]
    return x * c + pltpu.roll(x, LANES - half, 1) * sa + pltpu.roll(x, half, 1) * sb


def _ones_col(rows):
    lane = lax.broadcasted_iota(I32, (rows, LANES), 1)
    return jnp.where(lane == 0, 1.0, 0.0).astype(BF16)


def _lane_max2(mrun, s):
    return jnp.maximum(mrun, jnp.maximum(s[:, :LANES], s[:, LANES:]))


def _attend(qs, k_refs, va_refs, s_refs, qt, tk, *, diag_mask=None, mask_fn=None):
    n = len(qs)
    rows = qs[0].shape[0]
    static = isinstance(qt, int)

    if static:
        assert mask_fn is None
        w0 = qt * tk
        outs = []
        for h in range(n):
            sd = lax.dot_general(qs[h], k_refs[h][w0:w0 + tk, :], _NT, preferred_element_type=F32)
            parts = [sd if diag_mask is None else jnp.where(diag_mask, sd, NEG)]
            if qt > 0:
                parts.insert(0, lax.dot_general(qs[h], k_refs[h][0:w0, :], _NT, preferred_element_type=F32))
            m = functools.reduce(jnp.maximum, [jnp.max(p, axis=-1, keepdims=True) for p in parts])
            p = jnp.concatenate([jnp.exp2(p - m).astype(BF16) for p in parts], axis=1)
            acc = jnp.dot(p, va_refs[h][0:w0 + tk, :], preferred_element_type=F32)
            outs.append(acc[:, :LANES] / acc[:, LANES:LANES + 1])
        return outs

    def loop(lo, hi, body, init):
        return lax.fori_loop(lo, hi, body, init)

    def chunk(ref, c):
        return ref[pl.ds(pl.multiple_of(c * tk, tk), tk), :]

    def scores(h, c):
        s = lax.dot_general(qs[h], chunk(k_refs[h], c), _NT, preferred_element_type=F32)
        return s if mask_fn is None else mask_fn(c, s)

    def first(c, mruns):
        out = []
        for h in range(n):
            s = scores(h, c)
            s_refs[h][c] = s
            out.append(_lane_max2(mruns[h], s))
        return tuple(out)

    init = tuple(jnp.full((rows, LANES), NEG, F32) for _ in range(n))
    if diag_mask is None:
        mruns = loop(0, qt + 1, first, init)
    else:
        mruns = list(loop(0, qt, first, init))
        for h in range(n):
            s = jnp.where(diag_mask, scores(h, qt), NEG)
            s_refs[h][qt] = s
            mruns[h] = _lane_max2(mruns[h], s)
    ms = [jnp.max(mr, axis=-1, keepdims=True) for mr in mruns]

    def second(c, accs):
        return tuple(
            accs[h] + jnp.dot(jnp.exp2(s_refs[h][c] - ms[h]).astype(BF16), chunk(va_refs[h], c),
                              preferred_element_type=F32)
            for h in range(n))

    accs = loop(0, qt + 1, second, tuple(jnp.zeros((rows, 2 * LANES), F32) for _ in range(n)))
    return [a[:, :LANES] / a[:, LANES:LANES + 1] for a in accs]


def _attend_unrolled(nq, qt, finish, qs, k_refs, va_refs, s_refs, tk, **kwargs):
    for j in range(nq):
        pl.when(qt == j)(lambda j=j: finish(_attend(qs, k_refs, va_refs, s_refs, j, tk, **kwargs)))


def _norm_kernel(x_ref, g_ref, o_ref):
    o_ref[...] = _rms_full(x_ref[...], g_ref[...]).astype(o_ref.dtype)


def _rmsnorm_bf16(x, g):
    t, d = x.shape
    tm = 512
    return pl.pallas_call(
        _norm_kernel,
        grid=(t // tm,),
        in_specs=[pl.BlockSpec((tm, d), lambda i: (i, 0)), pl.BlockSpec((1, d), lambda i: (0, 0))],
        out_specs=pl.BlockSpec((tm, d), lambda i: (i, 0)),
        out_shape=jax.ShapeDtypeStruct((t, d), BF16),
        compiler_params=_cp(("parallel",)),
        name="rmsnorm",
    )(x, g.reshape(1, d))


def _cast_kernel(x_ref, o_ref):
    o_ref[...] = x_ref[...].astype(o_ref.dtype)


def _cast_bf16(w, layer, ncols):
    r = w.shape[1]
    tr, tc = 256, min(ncols, 1024)
    return pl.pallas_call(
        _cast_kernel,
        grid=(r // tr, ncols // tc),
        in_specs=[pl.BlockSpec((None, tr, tc), lambda i, j: (layer, i, j))],
        out_specs=pl.BlockSpec((tr, tc), lambda i, j: (i, j)),
        out_shape=jax.ShapeDtypeStruct((r, ncols), BF16),
        compiler_params=_cp(("parallel", "parallel")),
        name="cast_bf16",
    )(w)


def _matmul_kernel(x_ref, w_ref, o_ref):
    w = w_ref[...].astype(x_ref.dtype)
    o_ref[...] = jnp.dot(x_ref[...], w, preferred_element_type=F32).astype(o_ref.dtype)


def _matmul(x, w, layer, out_dtype, tm, tn, name):
    m, k = x.shape
    n = w.shape[2]
    return pl.pallas_call(
        _matmul_kernel,
        grid=(n // tn, m // tm),
        in_specs=[pl.BlockSpec((tm, k), lambda j, i: (i, 0)),
                  pl.BlockSpec((None, k, tn), lambda j, i: (layer, 0, j))],
        out_specs=pl.BlockSpec((tm, tn), lambda j, i: (i, j)),
        out_shape=jax.ShapeDtypeStruct((m, n), out_dtype),
        compiler_params=_cp(("parallel", "parallel")),
        name=name,
    )(x, w)


def _attn_a_kernel(*refs, tq, nq, nh, lam_init):
    q_refs, k_refs, v_refs = refs[0:nh], refs[nh:2 * nh], refs[2 * nh:3 * nh]
    tabq_ref, tabk_ref, qg_ref, kg_ref, lam_ref, sg_ref, o_ref, ks_ref, va_ref, s_ref = refs[3 * nh:]
    qt = pl.program_id(2)

    @pl.when(qt == 0)
    def _():
        ones = _ones_col(ks_ref.shape[1])
        for h in range(nh):
            k = _rms_halves(k_refs[h][...].astype(F32), kg_ref[...])
            ks_ref[h] = _rope(k, tabk_ref, 8).astype(BF16)
            va_ref[h, :, :LANES] = v_refs[h][...]
            va_ref[h, :, LANES:] = ones

    lane = lax.broadcasted_iota(I32, (tq, LANES), 1)
    qs = []
    for h in range(nh):
        q = _rms_halves(q_refs[h][...].astype(F32), qg_ref[...])
        q = _rope(q, tabq_ref, 8) * (A_QK ** -0.5 * LOG2E)
        qs.append(jnp.concatenate([jnp.where(lane < 64, q, 0.0), jnp.where(lane < 64, 0.0, q)],
                                  axis=0).astype(BF16))

    lp = lam_ref[...]
    lam = (jnp.exp(jnp.sum(lp[0:1] * lp[1:2], axis=-1, keepdims=True))
           - jnp.exp(jnp.sum(lp[2:3] * lp[3:4], axis=-1, keepdims=True)) + lam_init)

    row = lax.broadcasted_iota(I32, (2 * tq, tq), 0)
    row = jnp.where(row >= tq, row - tq, row)
    causal = lax.broadcasted_iota(I32, (2 * tq, tq), 1) <= row
    def finish(outs):
        for h in range(nh):
            o = outs[h][:tq] - lam * outs[h][tq:]
            o = _rms_full(o, sg_ref[...]) * (1.0 - lam_init)
            o_ref[:, h * HEAD_DIM:(h + 1) * HEAD_DIM] = o.astype(o_ref.dtype)

    _attend_unrolled(nq, qt, finish, qs, [ks_ref.at[h] for h in range(nh)], [va_ref.at[h] for h in range(nh)],
                     [s_ref.at[h] for h in range(nh)], tq, diag_mask=causal)


def _attn_a(proj, tab64, qg, kg, lam_p, sg, *, b, s, lam_init):
    tq = 256
    nq = s // tq
    nh = 3
    ng = A_HEADS // nh
    kern = functools.partial(_attn_a_kernel, tq=tq, nq=nq, nh=nh, lam_init=lam_init)
    small = lambda shape: pl.BlockSpec(shape, lambda bi, g, qt: (0,) * len(shape))
    qspec = lambda h: pl.BlockSpec((tq, LANES), lambda bi, g, qt: (bi * nq + qt, QA_C + g * nh + h))
    kspec = lambda c0, h: pl.BlockSpec((s, LANES), lambda bi, g, qt: (bi, c0 + g * nh + h))
    return pl.pallas_call(
        kern,
        grid=(b, ng, nq),
        in_specs=([qspec(h) for h in range(nh)] + [kspec(KA_C, h) for h in range(nh)]
                  + [kspec(VA_C, h) for h in range(nh)]
                  + [pl.BlockSpec((3, tq, LANES), lambda bi, g, qt: (0, qt, 0)),
                     pl.BlockSpec((3, s, LANES), lambda bi, g, qt: (0, 0, 0)),
                     small((1, LANES)), small((1, LANES)), small((4, A_QK)), small((1, LANES))]),
        out_specs=pl.BlockSpec((tq, nh * HEAD_DIM), lambda bi, g, qt: (bi * nq + qt, g)),
        out_shape=jax.ShapeDtypeStruct((b * s, A_HEADS * HEAD_DIM), BF16),
        scratch_shapes=[pltpu.VMEM((nh, s, LANES), BF16), pltpu.VMEM((nh, s, 2 * LANES), BF16),
                        pltpu.VMEM((nh, nq, 2 * tq, tq), F32)],
        compiler_params=_cp(("parallel", "parallel", "arbitrary")),
        name="attn_diff",
    )(*([proj] * (3 * nh)), tab64, tab64, qg, kg, lam_p, sg)


def _attn_b_kernel(*refs, tq, nblk):
    nh = B_HEADS
    q_refs, k_refs, v_refs = refs[0:nh], refs[nh:2 * nh], refs[2 * nh:3 * nh]
    tabq_ref, tabk_ref, qg_ref, kg_ref, o_ref, ksa_ref, va_ref, kmean_ref, s_ref = refs[3 * nh:]
    qt = pl.program_id(1)
    s_len = k_refs[0].shape[0]

    @pl.when(qt == 0)
    def _():
        blk = lax.shift_right_logical(lax.broadcasted_iota(I32, (s_len, LANES), 0), MOBA_BLOCK.bit_length() - 1)
        onehot = jnp.where(lax.broadcasted_iota(I32, (s_len, LANES), 1) == blk, 1.0, 0.0).astype(BF16)
        ones = _ones_col(s_len)
        for h in range(nh):
            k = _rope(_rms_full(k_refs[h][...].astype(F32), kg_ref[...]), tabk_ref, 16)
            ksa_ref[h, :, :LANES] = k.astype(BF16)
            ksa_ref[h, :, LANES:] = onehot
            kmean_ref[h] = jnp.zeros((LANES, LANES), F32)
            kmean_ref[h, 0:nblk, :] = jnp.mean(k.reshape(nblk, MOBA_BLOCK, HEAD_DIM), axis=1)
            va_ref[h, :, :LANES] = v_refs[h][...]
            va_ref[h, :, LANES:] = ones

    nbp = -(-nblk // 8) * 8
    blk = lax.broadcasted_iota(I32, (nbp, tq), 0)
    q_aug = []
    for h in range(nh):
        q = _rope(_rms_full(q_refs[h][...].astype(F32), qg_ref[...]), tabq_ref, 16)
        gate = lax.dot_general(kmean_ref[h, 0:nbp, :], q, _NT, preferred_element_type=F32,
                               precision=lax.Precision.HIGHEST)
        g = jnp.where(blk < qt, gate, -jnp.inf)
        keep = blk == qt
        for _ in range(MOBA_TOPK):
            mx = jnp.max(g, axis=0, keepdims=True)
            first = jnp.min(jnp.where(g == mx, blk, nbp), axis=0, keepdims=True)
            pick = jnp.logical_and(blk == first, mx > -jnp.inf)
            keep = jnp.logical_or(keep, pick)
            g = jnp.where(pick, -jnp.inf, g)
        bias_t = jnp.concatenate([jnp.where(keep, 0.0, NEG), jnp.zeros((LANES - nbp, tq), F32)], axis=0)
        q_aug.append(jnp.concatenate([q * (HEAD_DIM ** -0.5 * LOG2E), bias_t.T], axis=1).astype(BF16))

    causal = lax.broadcasted_iota(I32, (tq, tq), 1) <= lax.broadcasted_iota(I32, (tq, tq), 0)
    def finish(outs):
        for h in range(nh):
            o_ref[:, h * HEAD_DIM:(h + 1) * HEAD_DIM] = outs[h].astype(o_ref.dtype)

    _attend_unrolled(nblk, qt, finish, q_aug, [ksa_ref.at[h] for h in range(nh)],
                     [va_ref.at[h] for h in range(nh)], [s_ref.at[h] for h in range(nh)], tq, diag_mask=causal)


def _attn_b(proj, tab128, qg, kg, *, b, s):
    tq = MOBA_BLOCK
    nq = s // tq
    nh = B_HEADS
    kern = functools.partial(_attn_b_kernel, tq=tq, nblk=nq)
    small = lambda shape: pl.BlockSpec(shape, lambda bi, qt: (0,) * len(shape))
    qspec = lambda h: pl.BlockSpec((tq, LANES), lambda bi, qt: (bi * nq + qt, QB_C + h))
    kspec = lambda c0, h: pl.BlockSpec((s, LANES), lambda bi, qt: (bi, c0 + h))
    return pl.pallas_call(
        kern,
        grid=(b, nq),
        in_specs=([qspec(h) for h in range(nh)] + [kspec(KB_C, h) for h in range(nh)]
                  + [kspec(VB_C, h) for h in range(nh)]
                  + [pl.BlockSpec((3, tq, LANES), lambda bi, qt: (0, qt, 0)),
                     pl.BlockSpec((3, s, LANES), lambda bi, qt: (0, 0, 0)),
                     small((1, LANES)), small((1, LANES))]),
        out_specs=pl.BlockSpec((tq, nh * HEAD_DIM), lambda bi, qt: (bi * nq + qt, 0)),
        out_shape=jax.ShapeDtypeStruct((b * s, nh * HEAD_DIM), BF16),
        scratch_shapes=[pltpu.VMEM((nh, s, 2 * LANES), BF16), pltpu.VMEM((nh, s, 2 * LANES), BF16),
                        pltpu.VMEM((nh, LANES, LANES), F32), pltpu.VMEM((nh, nq, tq, tq), F32)],
        compiler_params=_cp(("parallel", "arbitrary")),
        name="attn_moba",
    )(*([proj] * (3 * nh)), tab128, tab128, qg, kg)


def _attn_c_kernel(q0_ref, q1_ref, q2_ref, q3_ref, q4_ref, qi_ref, kc_ref, vc_ref, kt_ref, wt_ref,
                   t128q_ref, t128k_ref, t64q_ref, t64k_ref, cqg_ref, ckg_ref, ikg_ref,
                   o_ref, kcs_ref, kis_ref, va_ref, keys_ref, s_ref, *, tq, tk, nq, n_sel):
    qt = pl.program_id(1)

    @pl.when(qt == 0)
    def _():
        k = _rope(_rms_full(kc_ref[...].astype(F32), ckg_ref[...]), t128k_ref, 16)
        kcs_ref[...] = k.astype(BF16)
        va_ref[:, :LANES] = vc_ref[...]
        va_ref[:, LANES:] = _ones_col(vc_ref.shape[0])
        t = kt_ref[...]
        lane = lax.broadcasted_iota(I32, t.shape, 1)
        ms = jnp.sum(jnp.where(lane < IDX_DIM, t * t, 0.0), axis=-1, keepdims=True) * (1.0 / IDX_DIM)
        ki = _rope(t * lax.rsqrt(ms + EPS) * ikg_ref[...], t64k_ref, 8)
        kis_ref[...] = (ki + pltpu.roll(ki, 64, 1)).astype(BF16)

    lane = lax.broadcasted_iota(I32, (tq, LANES), 1)
    heads = []
    for j in range(IDX_HEADS // 2):
        x = _rope(qi_ref[:, j * LANES:(j + 1) * LANES].astype(F32), t64q_ref, 8)
        heads.append(jnp.where(lane < 64, x, 0.0).astype(BF16))
        heads.append(jnp.where(lane < 64, 0.0, x).astype(BF16))
    hg = 4
    qi_groups = [jnp.concatenate(heads[g:g + hg], axis=0) for g in range(0, IDX_HEADS, hg)]
    wt = wt_ref[...] * ((IDX_HEADS ** -0.5) * (IDX_DIM ** -0.5))
    w_cols = [wt[:, IDX_DIM + h:IDX_DIM + h + 1] for h in range(IDX_HEADS)]

    row = qt * tq + lax.broadcasted_iota(I32, (tq, tk), 0)
    col0 = lax.broadcasted_iota(I32, (tq, tk), 1)
    nch = (qt + 1) * (tq // tk)

    def index_keys(c, _):
        off = pl.multiple_of(c * tk, tk)
        kchunk = kis_ref[pl.ds(off, tk), :]
        sc = jnp.zeros((tq, tk), F32)
        for g, qg in enumerate(qi_groups):
            lg = lax.dot_general(qg, kchunk, _NT, preferred_element_type=F32)
            for i in range(hg):
                sc = sc + jnp.maximum(lg[i * tq:(i + 1) * tq], 0.0) * w_cols[g * hg + i]
        keys_ref[c] = jnp.where(col0 + c * tk <= row, sc, -jnp.inf)
        return 0

    lax.fori_loop(0, nch, index_keys, 0)

    def as_float(k):
        return pltpu.bitcast(jnp.where(k < 0, k ^ jnp.int32(0x7FFFFFFF), k), F32)

    def count_ge(cand):
        cf = as_float(cand)

        def body(c, acc):
            a = jnp.where(keys_ref[c] >= cf, 1.0, 0.0)
            return acc + (a[:, :LANES] + a[:, LANES:])
        tot = lax.fori_loop(0, nch, body, jnp.zeros((tq, LANES), F32))
        return jnp.sum(tot, axis=-1, keepdims=True)

    lo = jnp.where(count_ge(jnp.zeros((tq, 1), I32)) >= n_sel, jnp.int32(0), jnp.int32(INT_MIN))

    def bit_step(i, lo):
        cand = lo + lax.shift_left(jnp.int32(1), 30 - i)
        return jnp.where(count_ge(cand) >= n_sel, cand, lo)

    lo = lax.fori_loop(0, 31, bit_step, lo)
    thr = as_float(jnp.maximum(lo, jnp.int32(INT_MIN + 0x00800000)))

    qs = []
    for q_ref in (q0_ref, q1_ref, q2_ref, q3_ref, q4_ref):
        q = _rope(_rms_full(q_ref[...].astype(F32), cqg_ref[...]), t128q_ref, 16)
        qs.append((q * (HEAD_DIM ** -0.5 * LOG2E)).astype(BF16))
    q_all = jnp.concatenate(qs, axis=0)
    rows = C_HEADS * tq

    def select(c, s):
        return jnp.where((keys_ref[c] >= thr)[None], s.reshape(C_HEADS, tq, tk), NEG).reshape(rows, tk)

    o, = _attend([q_all], [kcs_ref], [va_ref], [s_ref], nch - 1, tk, mask_fn=select)
    for h in range(C_HEADS):
        o_ref[:, h * HEAD_DIM:(h + 1) * HEAD_DIM] = o[h * tq:(h + 1) * tq].astype(o_ref.dtype)


def _attn_c(proj, tail, tab128, tab64, cqg, ckg, ikg, *, b, s):
    tq, tk = 256, 256
    nq = s // tq
    n_sel = min(DSA_TOPK, s // 4)
    assert tq % tk == 0 and s % tq == 0
    kern = functools.partial(_attn_c_kernel, tq=tq, tk=tk, nq=nq, n_sel=n_sel)
    small = lambda shape: pl.BlockSpec(shape, lambda bi, qt: (0,) * len(shape))
    qspec = lambda h: pl.BlockSpec((tq, LANES), lambda bi, qt: (bi * nq + qt, QC_C + h))
    return pl.pallas_call(
        kern,
        grid=(b, nq),
        in_specs=[
            qspec(0), qspec(1), qspec(2), qspec(3), qspec(4),
            pl.BlockSpec((tq, IDX_HEADS * IDX_DIM), lambda bi, qt: (bi * nq + qt, QI_C // 8)),
            pl.BlockSpec((s, LANES), lambda bi, qt: (bi, KC_C)),
            pl.BlockSpec((s, LANES), lambda bi, qt: (bi, VC_C)),
            pl.BlockSpec((s, LANES), lambda bi, qt: (bi, 0)),
            pl.BlockSpec((tq, LANES), lambda bi, qt: (bi * nq + qt, 0)),
            pl.BlockSpec((3, tq, LANES), lambda bi, qt: (0, qt, 0)),
            pl.BlockSpec((3, s, LANES), lambda bi, qt: (0, 0, 0)),
            pl.BlockSpec((3, tq, LANES), lambda bi, qt: (0, qt, 0)),
            pl.BlockSpec((3, s, LANES), lambda bi, qt: (0, 0, 0)),
            small((1, LANES)), small((1, LANES)), small((1, LANES)),
        ],
        out_specs=pl.BlockSpec((tq, C_HEADS * HEAD_DIM), lambda bi, qt: (bi * nq + qt, 0)),
        out_shape=jax.ShapeDtypeStruct((b * s, C_HEADS * HEAD_DIM), BF16),
        scratch_shapes=[pltpu.VMEM((s, LANES), BF16), pltpu.VMEM((s, LANES), BF16),
                        pltpu.VMEM((s, 2 * LANES), BF16), pltpu.VMEM((s // tk, tq, tk), F32),
                        pltpu.VMEM((s // tk, C_HEADS * tq, tk), F32)],
        compiler_params=_cp(("parallel", "arbitrary")),
        name="attn_dsa",
    )(proj, proj, proj, proj, proj, proj, proj, proj, tail, tail,
      tab128, tab128, tab64, tab64, cqg, ckg, ikg)


def _out_router_kernel(x_ref, oa_ref, ob_ref, oc_ref, w_ref, g2_ref, wr_ref, br_ref,
                       x1_ref, t_ref, eid_ref, gate_ref):
    na = A_HEADS * HEAD_DIM
    nb = na + B_HEADS * HEAD_DIM
    x1 = (x_ref[...]
          + jnp.dot(oa_ref[...], w_ref[0:na, :], preferred_element_type=F32)
          + jnp.dot(ob_ref[...], w_ref[na:nb, :], preferred_element_type=F32)
          + jnp.dot(oc_ref[...], w_ref[nb:, :], preferred_element_type=F32))
    x1_ref[...] = x1
    t = _rms_full(x1, g2_ref[...])
    t_ref[...] = t

    lg = jnp.dot(t, wr_ref[...], preferred_element_type=F32, precision=lax.Precision.HIGHEST) + br_ref[...]
    lane = lax.broadcasted_iota(I32, lg.shape, 1)
    ninf = -jnp.inf
    gl = jnp.where(lane < N_GROUPS, lg, ninf)
    gm = jnp.max(gl, axis=-1, keepdims=True)
    ge = jnp.exp(gl - gm)
    g_prob = ge / jnp.sum(ge, axis=-1, keepdims=True)
    g_idx = jnp.min(jnp.where(gl == gm, lane, LANES), axis=-1, keepdims=True)
    g_w = jnp.sum(jnp.where(lane == g_idx, g_prob, 0.0), axis=-1, keepdims=True)

    e0 = N_GROUPS + g_idx * EXPERTS_PER_GROUP
    emask = jnp.logical_and(lane >= e0, lane < e0 + EXPERTS_PER_GROUP)
    el = jnp.where(emask, lg, ninf)
    em = jnp.max(el, axis=-1, keepdims=True)
    ee = jnp.exp(el - em)
    ep = jnp.where(emask, ee / jnp.sum(ee, axis=-1, keepdims=True), ninf)
    v1 = jnp.max(ep, axis=-1, keepdims=True)
    i1 = jnp.min(jnp.where(ep == v1, lane, LANES), axis=-1, keepdims=True)
    ep2 = jnp.where(lane == i1, ninf, ep)
    v2 = jnp.max(ep2, axis=-1, keepdims=True)
    i2 = jnp.min(jnp.where(ep2 == v2, lane, LANES), axis=-1, keepdims=True)
    den = v1 + v2
    eid_ref[...] = jnp.where(lane == 0, i1 - N_GROUPS, jnp.where(lane == 1, i2 - N_GROUPS, 0))
    gate_ref[...] = jnp.where(lane == 0, g_w * (v1 / den), jnp.where(lane == 1, g_w * (v2 / den), 0.0))


def _out_router(x, oa, ob, oc, w_out_bf, g2, w_router, b_router):
    t, d = x.shape
    tm = 256
    row = lambda c: pl.BlockSpec((tm, c), lambda i: (i, 0))
    full = lambda r, c: pl.BlockSpec((r, c), lambda i: (0, 0))
    return pl.pallas_call(
        _out_router_kernel,
        grid=(t // tm,),
        in_specs=[row(d), row(oa.shape[1]), row(ob.shape[1]), row(oc.shape[1]),
                  full(d, d), full(1, d), full(d, LANES), full(1, LANES)],
        out_specs=[row(d), row(d), row(LANES), row(LANES)],
        out_shape=[jax.ShapeDtypeStruct((t, d), F32), jax.ShapeDtypeStruct((t, d), F32),
                   jax.ShapeDtypeStruct((t, LANES), I32), jax.ShapeDtypeStruct((t, LANES), F32)],
        compiler_params=_cp(("parallel",)),
        name="out_proj_router",
    )(x, oa, ob, oc, w_out_bf, g2.reshape(1, d), w_router, b_router)


def _moe_kernel(texp_ref, nused_ref, rtok_ref, t_hbm, wg_ref, wu_ref, wd_ref,
                y_ref, xbuf, sem, wg_bf, wu_bf, wd_bf, *, tm):
    i = pl.program_id(0)
    nu = nused_ref[0]

    def gather(tile, slot):
        def body(r, _):
            tok = rtok_ref[tile * tm + r]
            pltpu.make_async_copy(t_hbm.at[pl.ds(tok, 1)], xbuf.at[slot, pl.ds(r, 1)], sem.at[slot]).start()
            return 0
        lax.fori_loop(0, tm, body, 0, unroll=8)

    @pl.when(i == 0)
    def _():
        gather(0, 0)

    @pl.when(i + 1 < nu)
    def _():
        gather(i + 1, (i + 1) % 2)

    @pl.when(i < nu)
    def _():
        slot = i % 2
        pltpu.make_async_copy(xbuf.at[slot], xbuf.at[slot], sem.at[slot]).wait()

        changed = jnp.logical_or(i == 0, texp_ref[i] != texp_ref[jnp.maximum(i - 1, 0)])

        @pl.when(changed)
        def _():
            wg_bf[...] = wg_ref[...].astype(BF16)
            wu_bf[...] = wu_ref[...].astype(BF16)
            wd_bf[...] = wd_ref[...].astype(BF16)

        x = xbuf[slot].astype(BF16)
        g = jnp.dot(x, wg_bf[...], preferred_element_type=F32)
        u = jnp.dot(x, wu_bf[...], preferred_element_type=F32)
        h = (g * jax.nn.sigmoid(g)) * u
        y_ref[...] = jnp.dot(h.astype(BF16), wd_bf[...], preferred_element_type=F32)

    @pl.when(i >= nu)
    def _():
        y_ref[...] = jnp.zeros_like(y_ref)


def _moe(t, tile_expert, n_used, row_token, w_gate, w_up, w_down, *, layer, n_tiles):
    tm = MOE_TILE
    d = t.shape[1]
    grid_spec = pltpu.PrefetchScalarGridSpec(
        num_scalar_prefetch=3,
        grid=(n_tiles,),
        in_specs=[
            pl.BlockSpec(memory_space=pl.ANY),
            pl.BlockSpec((None, None, d, EXPERT_FF), lambda i, te, nu, rt: (layer, te[i], 0, 0)),
            pl.BlockSpec((None, None, d, EXPERT_FF), lambda i, te, nu, rt: (layer, te[i], 0, 0)),
            pl.BlockSpec((None, None, EXPERT_FF, d), lambda i, te, nu, rt: (layer, te[i], 0, 0)),
        ],
        out_specs=pl.BlockSpec((tm, d), lambda i, te, nu, rt: (i, 0)),
        scratch_shapes=[pltpu.VMEM((2, tm, d), F32), pltpu.SemaphoreType.DMA((2,)),
                        pltpu.VMEM((d, EXPERT_FF), BF16), pltpu.VMEM((d, EXPERT_FF), BF16),
                        pltpu.VMEM((EXPERT_FF, d), BF16)],
    )
    return pl.pallas_call(
        functools.partial(_moe_kernel, tm=tm),
        grid_spec=grid_spec,
        out_shape=jax.ShapeDtypeStruct((n_tiles * tm, d), F32),
        compiler_params=_cp(("arbitrary",)),
        name="moe_experts",
    )(tile_expert, n_used, row_token, t, w_gate, w_up, w_down)


def _combine_kernel(pos_ref, x1_ref, gate_ref, gn_ref, y_hbm, o_ref, *rest, tc):
    h_ref = rest[0] if len(rest) == 3 else None
    buf, sem = rest[-2:]
    i = pl.program_id(0)
    n = pl.num_programs(0)

    def gather(tile, slot):
        def body(r, _):
            a = (tile * tc + r) * 2
            pltpu.make_async_copy(y_hbm.at[pl.ds(pos_ref[a], 1)], buf.at[slot, 0, pl.ds(r, 1)], sem.at[slot]).start()
            pltpu.make_async_copy(y_hbm.at[pl.ds(pos_ref[a + 1], 1)], buf.at[slot, 1, pl.ds(r, 1)], sem.at[slot]).start()
            return 0
        lax.fori_loop(0, tc, body, 0, unroll=8)

    @pl.when(i == 0)
    def _():
        gather(0, 0)

    @pl.when(i + 1 < n)
    def _():
        gather(i + 1, (i + 1) % 2)

    slot = i % 2
    pltpu.make_async_copy(buf.at[slot], buf.at[slot], sem.at[slot]).wait()
    gt = gate_ref[...]
    x2 = x1_ref[...] + gt[:, 0:1] * buf[slot, 0] + gt[:, 1:2] * buf[slot, 1]
    o_ref[...] = x2
    if h_ref is not None:
        h_ref[...] = _rms_full(x2, gn_ref[...]).astype(h_ref.dtype)


def _combine(pos, x1, gate, y, g_next, emit_h):
    t, d = x1.shape
    tc = 256
    row = pl.BlockSpec((tc, d), lambda i, p: (i, 0))
    grid_spec = pltpu.PrefetchScalarGridSpec(
        num_scalar_prefetch=1,
        grid=(t // tc,),
        in_specs=[row, pl.BlockSpec((tc, LANES), lambda i, p: (i, 0)), pl.BlockSpec((1, d), lambda i, p: (0, 0)),
                  pl.BlockSpec(memory_space=pl.ANY)],
        out_specs=[row, row] if emit_h else row,
        scratch_shapes=[pltpu.VMEM((2, 2, tc, d), F32), pltpu.SemaphoreType.DMA((2,))],
    )
    x2_shape = jax.ShapeDtypeStruct((t, d), F32)
    return pl.pallas_call(
        functools.partial(_combine_kernel, tc=tc),
        grid_spec=grid_spec,
        out_shape=[x2_shape, jax.ShapeDtypeStruct((t, d), BF16)] if emit_h else x2_shape,
        compiler_params=_cp(("arbitrary",)),
        name="moe_combine",
    )(pos, x1, gate, g_next.reshape(1, d), y)


def _rope_lane_tables(seq, dim):
    rot = dim // ROPE_FRAC
    half = rot // 2
    inv = 1.0 / (ROPE_THETA ** (jnp.arange(0, rot, 2, dtype=F32) / rot))
    ang = jnp.arange(seq, dtype=F32)[:, None] * inv[None, :]
    cos, sin = jnp.cos(ang), jnp.sin(ang)
    z_half = jnp.zeros((seq, half), F32)
    z_rest = jnp.zeros((seq, dim - rot), F32)
    c = jnp.concatenate([cos, cos, jnp.ones((seq, dim - rot), F32)], axis=-1)
    sa = jnp.concatenate([-sin, z_half, z_rest], axis=-1)
    sb = jnp.concatenate([z_half, sin, z_rest], axis=-1)
    reps = LANES // dim
    return jnp.stack([jnp.tile(c, (1, reps)), jnp.tile(sa, (1, reps)), jnp.tile(sb, (1, reps))])


def _routing_plan(eid, n_tiles):
    tm = MOE_TILE
    e = eid[:, :2].reshape(-1)
    n_assign = e.shape[0]
    onehot = (e[:, None] == jnp.arange(N_EXPERTS, dtype=I32)[None, :]).astype(I32)
    csum = jnp.cumsum(onehot, axis=0)
    rank = jnp.take_along_axis(csum, e[:, None], axis=1)[:, 0] - 1
    counts = csum[-1]
    tiles_per = (counts + tm - 1) // tm
    tile_end = jnp.cumsum(tiles_per)
    tile_start = tile_end - tiles_per
    n_used = tile_end[-1]
    pos = tile_start[e] * tm + rank
    tile_ids = jnp.minimum(jnp.arange(n_tiles, dtype=I32), n_used - 1)
    tile_expert = jnp.sum((tile_end[None, :] <= tile_ids[:, None]).astype(I32), axis=1)
    row_token = jnp.zeros((n_tiles * tm,), I32).at[pos].set(jnp.arange(n_assign, dtype=I32) // 2)
    return tile_expert, n_used.reshape(1).astype(I32), row_token, pos.astype(I32)


def kernel(x, norm1_g, w_in, a_qn_g, a_kn_g, a_lambda, a_subln_g, b_qn_g, b_kn_g, c_qn_g, c_kn_g,
           idx_kn_g, w_out, norm2_g, w_group, b_group, w_expert, b_expert, w_gate, w_up, w_down):
    b, s, d = x.shape
    depth = w_in.shape[0]
    assert d == D_MODEL and s % MOBA_BLOCK == 0
    t = b * s
    n_tiles = (2 * t) // MOE_TILE + N_EXPERTS
    tab64 = _rope_lane_tables(s, A_QK)
    tab128 = _rope_lane_tables(s, HEAD_DIM)
    tile2 = lambda v: jnp.tile(v, 2).reshape(1, LANES)
    row = lambda v: v.reshape(1, LANES)

    w_main = w_in[:, :, :MAIN_COLS].astype(BF16)
    w_tail = jnp.pad(w_in[:, :, MAIN_COLS:], ((0, 0), (0, 0), (0, LANES - TAIL_COLS)))

    xf = x.reshape(t, d)
    h = _rmsnorm_bf16(xf, norm1_g[0])
    for l in range(depth):
        lam_init = 0.8 - 0.6 * math.exp(-0.3 * l)
        proj = _matmul(h, w_main, l, BF16, 1024, 1024, "in_proj")
        tail = _matmul(h, w_tail, l, F32, 1024, LANES, "in_proj_tail")

        oa = _attn_a(proj, tab64, tile2(a_qn_g[l]), tile2(a_kn_g[l]), a_lambda[l], row(a_subln_g[l]),
                     b=b, s=s, lam_init=lam_init)
        ob = _attn_b(proj, tab128, row(b_qn_g[l]), row(b_kn_g[l]), b=b, s=s)
        ikg = jnp.pad(idx_kn_g[l], (0, LANES - IDX_DIM)).reshape(1, LANES)
        oc = _attn_c(proj, tail, tab128, tab64, row(c_qn_g[l]), row(c_kn_g[l]), ikg, b=b, s=s)

        w_router = jnp.pad(jnp.concatenate([w_group[l], w_expert[l]], axis=1),
                           ((0, 0), (0, LANES - N_GROUPS - N_EXPERTS)))
        b_router = jnp.pad(jnp.concatenate([b_group[l], b_expert[l]]),
                           (0, LANES - N_GROUPS - N_EXPERTS)).reshape(1, LANES)
        x1, tn, eid, gate = _out_router(xf, oa, ob, oc, _cast_bf16(w_out, l, d), norm2_g[l], w_router, b_router)

        tile_expert, n_used, row_token, pos = _routing_plan(eid, n_tiles)
        y = _moe(tn, tile_expert, n_used, row_token, w_gate, w_up, w_down, layer=l, n_tiles=n_tiles)
        if l + 1 < depth:
            xf, h = _combine(pos, x1, gate, y, norm1_g[l + 1], True)
        else:
            xf = _combine(pos, x1, gate, y, norm1_g[l], False)
    return xf.reshape(b, s, d)
```

```python
import functools
import math

import jax
import jax.numpy as jnp
from jax import lax
from jax.experimental import pallas as pl
from jax.experimental.pallas import tpu as pltpu

F32 = jnp.float32
BF16 = jnp.bfloat16
I32 = jnp.int32

D_MODEL = 2048
HEAD_DIM = 128
A_HEADS = 6
A_QK = 64
B_HEADS = 5
MOBA_BLOCK = 256
MOBA_TOPK = 3
C_HEADS = 5
IDX_HEADS = 16
IDX_DIM = 64
DSA_TOPK = 256
ROPE_THETA = 500000.0
ROPE_FRAC = 4
EPS = 1e-6
N_GROUPS = 4
EXPERTS_PER_GROUP = 8
N_EXPERTS = N_GROUPS * EXPERTS_PER_GROUP
EXPERT_FF = 512

LANES = 128
MAIN_COLS = 6144
TAIL_COLS = 80
QA_C, KA_C, VA_C = 0, 6, 12
QB_C, KB_C, VB_C = 18, 23, 28
QC_C, KC_C, VC_C = 33, 38, 39
QI_C = 40
NEG = -1e30
LOG2E = 1.4426950408889634
INT_MIN = -(2 ** 31)
MOE_TILE = 256
VMEM_LIMIT = 56 * 1024 * 1024

_NT = (((1,), (1,)), ((), ()))


def _cp(sem, vmem=VMEM_LIMIT):
    return pltpu.CompilerParams(dimension_semantics=sem, vmem_limit_bytes=vmem)


def _rms_full(x, g):
    ms = jnp.mean(x * x, axis=-1, keepdims=True)
    return x * lax.rsqrt(ms + EPS) * g


def _rms_halves(x, g):
    lane = lax.broadcasted_iota(I32, x.shape, 1)
    lo = lane < 64
    x2 = x * x
    s_lo = jnp.sum(jnp.where(lo, x2, 0.0), axis=-1, keepdims=True)
    s_hi = jnp.sum(jnp.where(lo, 0.0, x2), axis=-1, keepdims=True)
    r = jnp.where(lo, lax.rsqrt(s_lo * (1.0 / 64) + EPS), lax.rsqrt(s_hi * (1.0 / 64) + EPS))
    return x * r * g


def _rope(x, tab_ref, half):
    c = tab_ref[0]
    sa = tab_ref[1]
    sb = tab_ref[2]
    return x * c + pltpu.roll(x, LANES - half, 1) * sa + pltpu.roll(x, half, 1) * sb


def _ones_col(rows):
    lane = lax.broadcasted_iota(I32, (rows, LANES), 1)
    return jnp.where(lane == 0, 1.0, 0.0).astype(BF16)


def _lane_max2(mrun, s):
    return jnp.maximum(mrun, jnp.maximum(s[:, :LANES], s[:, LANES:]))


def _attend(qs, k_refs, va_refs, s_refs, qt, tk, *, diag_mask=None, mask_fn=None):
    n = len(qs)
    rows = qs[0].shape[0]
    static = isinstance(qt, int)

    if static:
        assert mask_fn is None
        w0 = qt * tk
        outs = []
        for h in range(n):
            sd = lax.dot_general(qs[h], k_refs[h][w0:w0 + tk, :], _NT, preferred_element_type=F32)
            parts = [sd if diag_mask is None else jnp.where(diag_mask, sd, NEG)]
            if qt > 0:
                parts.insert(0, lax.dot_general(qs[h], k_refs[h][0:w0, :], _NT, preferred_element_type=F32))
            m = functools.reduce(jnp.maximum, [jnp.max(p, axis=-1, keepdims=True) for p in parts])
            p = jnp.concatenate([jnp.exp2(p - m).astype(BF16) for p in parts], axis=1)
            acc = jnp.dot(p, va_refs[h][0:w0 + tk, :], preferred_element_type=F32)
            outs.append(acc[:, :LANES] / acc[:, LANES:LANES + 1])
        return outs

    def loop(lo, hi, body, init):
        return lax.fori_loop(lo, hi, body, init)

    def chunk(ref, c):
        return ref[pl.ds(pl.multiple_of(c * tk, tk), tk), :]

    def scores(h, c):
        s = lax.dot_general(qs[h], chunk(k_refs[h], c), _NT, preferred_element_type=F32)
        return s if mask_fn is None else mask_fn(c, s)

    def first(c, mruns):
        out = []
        for h in range(n):
            s = scores(h, c)
            s_refs[h][c] = s
            out.append(_lane_max2(mruns[h], s))
        return tuple(out)

    init = tuple(jnp.full((rows, LANES), NEG, F32) for _ in range(n))
    if diag_mask is None:
        mruns = loop(0, qt + 1, first, init)
    else:
        mruns = list(loop(0, qt, first, init))
        for h in range(n):
            s = jnp.where(diag_mask, scores(h, qt), NEG)
            s_refs[h][qt] = s
            mruns[h] = _lane_max2(mruns[h], s)
    ms = [jnp.max(mr, axis=-1, keepdims=True) for mr in mruns]

    def second(c, accs):
        return tuple(
            accs[h] + jnp.dot(jnp.exp2(s_refs[h][c] - ms[h]).astype(BF16), chunk(va_refs[h], c),
                              preferred_element_type=F32)
            for h in range(n))

    accs = loop(0, qt + 1, second, tuple(jnp.zeros((rows, 2 * LANES), F32) for _ in range(n)))
    return [a[:, :LANES] / a[:, LANES:LANES + 1] for a in accs]


def _attend_unrolled(nq, qt, finish, qs, k_refs, va_refs, s_refs, tk, **kwargs):
    for j in range(nq):
        pl.when(qt == j)(lambda j=j: finish(_attend(qs, k_refs, va_refs, s_refs, j, tk, **kwargs)))


def _norm_kernel(x_ref, g_ref, o_ref):
    o_ref[...] = _rms_full(x_ref[...], g_ref[...]).astype(o_ref.dtype)


def _rmsnorm_bf16(x, g):
    t, d = x.shape
    tm = 512
    return pl.pallas_call(
        _norm_kernel,
        grid=(t // tm,),
        in_specs=[pl.BlockSpec((tm, d), lambda i: (i, 0)), pl.BlockSpec((1, d), lambda i: (0, 0))],
        out_specs=pl.BlockSpec((tm, d), lambda i: (i, 0)),
        out_shape=jax.ShapeDtypeStruct((t, d), BF16),
        compiler_params=_cp(("parallel",)),
        name="rmsnorm",
    )(x, g.reshape(1, d))


def _cast_kernel(x_ref, o_ref):
    o_ref[...] = x_ref[...].astype(o_ref.dtype)


def _cast_bf16(w, layer, ncols):
    r = w.shape[1]
    tr, tc = 256, min(ncols, 1024)
    return pl.pallas_call(
        _cast_kernel,
        grid=(r // tr, ncols // tc),
        in_specs=[pl.BlockSpec((None, tr, tc), lambda i, j: (layer, i, j))],
        out_specs=pl.BlockSpec((tr, tc), lambda i, j: (i, j)),
        out_shape=jax.ShapeDtypeStruct((r, ncols), BF16),
        compiler_params=_cp(("parallel", "parallel")),
        name="cast_bf16",
    )(w)


def _matmul_kernel(x_ref, w_ref, o_ref):
    w = w_ref[...].astype(x_ref.dtype)
    o_ref[...] = jnp.dot(x_ref[...], w, preferred_element_type=F32).astype(o_ref.dtype)


def _matmul(x, w, layer, out_dtype, tm, tn, name):
    m, k = x.shape
    n = w.shape[2]
    return pl.pallas_call(
        _matmul_kernel,
        grid=(n // tn, m // tm),
        in_specs=[pl.BlockSpec((tm, k), lambda j, i: (i, 0)),
                  pl.BlockSpec((None, k, tn), lambda j, i: (layer, 0, j))],
        out_specs=pl.BlockSpec((tm, tn), lambda j, i: (i, j)),
        out_shape=jax.ShapeDtypeStruct((m, n), out_dtype),
        compiler_params=_cp(("parallel", "parallel")),
        name=name,
    )(x, w)


def _attn_a_kernel(*refs, tq, nq, nh, lam_init):
    q_refs, k_refs, v_refs = refs[0:nh], refs[nh:2 * nh], refs[2 * nh:3 * nh]
    tabq_ref, tabk_ref, qg_ref, kg_ref, lam_ref, sg_ref, o_ref, ks_ref, va_ref, s_ref = refs[3 * nh:]
    qt = pl.program_id(2)

    @pl.when(qt == 0)
    def _():
        ones = _ones_col(ks_ref.shape[1])
        for h in range(nh):
            k = _rms_halves(k_refs[h][...].astype(F32), kg_ref[...])
            ks_ref[h] = _rope(k, tabk_ref, 8).astype(BF16)
            va_ref[h, :, :LANES] = v_refs[h][...]
            va_ref[h, :, LANES:] = ones

    lane = lax.broadcasted_iota(I32, (tq, LANES), 1)
    qs = []
    for h in range(nh):
        q = _rms_halves(q_refs[h][...].astype(F32), qg_ref[...])
        q = _rope(q, tabq_ref, 8) * (A_QK ** -0.5 * LOG2E)
        qs.append(jnp.concatenate([jnp.where(lane < 64, q, 0.0), jnp.where(lane < 64, 0.0, q)],
                                  axis=0).astype(BF16))

    lp = lam_ref[...]
    lam = (jnp.exp(jnp.sum(lp[0:1] * lp[1:2], axis=-1, keepdims=True))
           - jnp.exp(jnp.sum(lp[2:3] * lp[3:4], axis=-1, keepdims=True)) + lam_init)

    row = lax.broadcasted_iota(I32, (2 * tq, tq), 0)
    row = jnp.where(row >= tq, row - tq, row)
    causal = lax.broadcasted_iota(I32, (2 * tq, tq), 1) <= row
    def finish(outs):
        for h in range(nh):
            o = outs[h][:tq] - lam * outs[h][tq:]
            o = _rms_full(o, sg_ref[...]) * (1.0 - lam_init)
            o_ref[:, h * HEAD_DIM:(h + 1) * HEAD_DIM] = o.astype(o_ref.dtype)

    _attend_unrolled(nq, qt, finish, qs, [ks_ref.at[h] for h in range(nh)], [va_ref.at[h] for h in range(nh)],
                     [s_ref.at[h] for h in range(nh)], tq, diag_mask=causal)


def _attn_a(proj, tab64, qg, kg, lam_p, sg, *, b, s, lam_init):
    tq = 256
    nq = s // tq
    nh = 3
    ng = A_HEADS // nh
    kern = functools.partial(_attn_a_kernel, tq=tq, nq=nq, nh=nh, lam_init=lam_init)
    small = lambda shape: pl.BlockSpec(shape, lambda bi, g, qt: (0,) * len(shape))
    qspec = lambda h: pl.BlockSpec((tq, LANES), lambda bi, g, qt: (bi * nq + qt, QA_C + g * nh + h))
    kspec = lambda c0, h: pl.BlockSpec((s, LANES), lambda bi, g, qt: (bi, c0 + g * nh + h))
    return pl.pallas_call(
        kern,
        grid=(b, ng, nq),
        in_specs=([qspec(h) for h in range(nh)] + [kspec(KA_C, h) for h in range(nh)]
                  + [kspec(VA_C, h) for h in range(nh)]
                  + [pl.BlockSpec((3, tq, LANES), lambda bi, g, qt: (0, qt, 0)),
                     pl.BlockSpec((3, s, LANES), lambda bi, g, qt: (0, 0, 0)),
                     small((1, LANES)), small((1, LANES)), small((4, A_QK)), small((1, LANES))]),
        out_specs=pl.BlockSpec((tq, nh * HEAD_DIM), lambda bi, g, qt: (bi * nq + qt, g)),
        out_shape=jax.ShapeDtypeStruct((b * s, A_HEADS * HEAD_DIM), BF16),
        scratch_shapes=[pltpu.VMEM((nh, s, LANES), BF16), pltpu.VMEM((nh, s, 2 * LANES), BF16),
                        pltpu.VMEM((nh, nq, 2 * tq, tq), F32)],
        compiler_params=_cp(("parallel", "parallel", "arbitrary")),
        name="attn_diff",
    )(*([proj] * (3 * nh)), tab64, tab64, qg, kg, lam_p, sg)


def _attn_b_kernel(*refs, tq, nblk):
    nh = B_HEADS
    q_refs, k_refs, v_refs = refs[0:nh], refs[nh:2 * nh], refs[2 * nh:3 * nh]
    tabq_ref, tabk_ref, qg_ref, kg_ref, o_ref, ksa_ref, va_ref, kmean_ref, s_ref = refs[3 * nh:]
    qt = pl.program_id(1)
    s_len = k_refs[0].shape[0]

    @pl.when(qt == 0)
    def _():
        blk = lax.shift_right_logical(lax.broadcasted_iota(I32, (s_len, LANES), 0), MOBA_BLOCK.bit_length() - 1)
        onehot = jnp.where(lax.broadcasted_iota(I32, (s_len, LANES), 1) == blk, 1.0, 0.0).astype(BF16)
        ones = _ones_col(s_len)
        for h in range(nh):
            k = _rope(_rms_full(k_refs[h][...].astype(F32), kg_ref[...]), tabk_ref, 16)
            ksa_ref[h, :, :LANES] = k.astype(BF16)
            ksa_ref[h, :, LANES:] = onehot
            kmean_ref[h] = jnp.zeros((LANES, LANES), F32)
            kmean_ref[h, 0:nblk, :] = jnp.mean(k.reshape(nblk, MOBA_BLOCK, HEAD_DIM), axis=1)
            va_ref[h, :, :LANES] = v_refs[h][...]
            va_ref[h, :, LANES:] = ones

    nbp = -(-nblk // 8) * 8
    blk = lax.broadcasted_iota(I32, (nbp, tq), 0)
    q_aug = []
    for h in range(nh):
        q = _rope(_rms_full(q_refs[h][...].astype(F32), qg_ref[...]), tabq_ref, 16)
        gate = lax.dot_general(kmean_ref[h, 0:nbp, :], q, _NT, preferred_element_type=F32,
                               precision=lax.Precision.HIGHEST)
        g = jnp.where(blk < qt, gate, -jnp.inf)
        keep = blk == qt
        for _ in range(MOBA_TOPK):
            mx = jnp.max(g, axis=0, keepdims=True)
            first = jnp.min(jnp.where(g == mx, blk, nbp), axis=0, keepdims=True)
            pick = jnp.logical_and(blk == first, mx > -jnp.inf)
            keep = jnp.logical_or(keep, pick)
            g = jnp.where(pick, -jnp.inf, g)
        bias_t = jnp.concatenate([jnp.where(keep, 0.0, NEG), jnp.zeros((LANES - nbp, tq), F32)], axis=0)
        q_aug.append(jnp.concatenate([q * (HEAD_DIM ** -0.5 * LOG2E), bias_t.T], axis=1).astype(BF16))

    causal = lax.broadcasted_iota(I32, (tq, tq), 1) <= lax.broadcasted_iota(I32, (tq, tq), 0)
    def finish(outs):
        for h in range(nh):
            o_ref[:, h * HEAD_DIM:(h + 1) * HEAD_DIM] = outs[h].astype(o_ref.dtype)

    _attend_unrolled(nblk, qt, finish, q_aug, [ksa_ref.at[h] for h in range(nh)],
                     [va_ref.at[h] for h in range(nh)], [s_ref.at[h] for h in range(nh)], tq, diag_mask=causal)


def _attn_b(proj, tab128, qg, kg, *, b, s):
    tq = MOBA_BLOCK
    nq = s // tq
    nh = B_HEADS
    kern = functools.partial(_attn_b_kernel, tq=tq, nblk=nq)
    small = lambda shape: pl.BlockSpec(shape, lambda bi, qt: (0,) * len(shape))
    qspec = lambda h: pl.BlockSpec((tq, LANES), lambda bi, qt: (bi * nq + qt, QB_C + h))
    kspec = lambda c0, h: pl.BlockSpec((s, LANES), lambda bi, qt: (bi, c0 + h))
    return pl.pallas_call(
        kern,
        grid=(b, nq),
        in_specs=([qspec(h) for h in range(nh)] + [kspec(KB_C, h) for h in range(nh)]
                  + [kspec(VB_C, h) for h in range(nh)]
                  + [pl.BlockSpec((3, tq, LANES), lambda bi, qt: (0, qt, 0)),
                     pl.BlockSpec((3, s, LANES), lambda bi, qt: (0, 0, 0)),
                     small((1, LANES)), small((1, LANES))]),
        out_specs=pl.BlockSpec((tq, nh * HEAD_DIM), lambda bi, qt: (bi * nq + qt, 0)),
        out_shape=jax.ShapeDtypeStruct((b * s, nh * HEAD_DIM), BF16),
        scratch_shapes=[pltpu.VMEM((nh, s, 2 * LANES), BF16), pltpu.VMEM((nh, s, 2 * LANES), BF16),
                        pltpu.VMEM((nh, LANES, LANES), F32), pltpu.VMEM((nh, nq, tq, tq), F32)],
        compiler_params=_cp(("parallel", "arbitrary")),
        name="attn_moba",
    )(*([proj] * (3 * nh)), tab128, tab128, qg, kg)


def _attn_c_kernel(q0_ref, q1_ref, q2_ref, q3_ref, q4_ref, qi_ref, kc_ref, vc_ref, kt_ref, wt_ref,
                   t128q_ref, t128k_ref, t64q_ref, t64k_ref, cqg_ref, ckg_ref, ikg_ref,
                   o_ref, kcs_ref, kis_ref, va_ref, keys_ref, s_ref, *, tq, tk, nq, n_sel):
    qt = pl.program_id(1)

    @pl.when(qt == 0)
    def _():
        k = _rope(_rms_full(kc_ref[...].astype(F32), ckg_ref[...]), t128k_ref, 16)
        kcs_ref[...] = k.astype(BF16)
        va_ref[:, :LANES] = vc_ref[...]
        va_ref[:, LANES:] = _ones_col(vc_ref.shape[0])
        t = kt_ref[...]
        lane = lax.broadcasted_iota(I32, t.shape, 1)
        ms = jnp.sum(jnp.where(lane < IDX_DIM, t * t, 0.0), axis=-1, keepdims=True) * (1.0 / IDX_DIM)
        ki = _rope(t * lax.rsqrt(ms + EPS) * ikg_ref[...], t64k_ref, 8)
        kis_ref[...] = (ki + pltpu.roll(ki, 64, 1)).astype(BF16)

    lane = lax.broadcasted_iota(I32, (tq, LANES), 1)
    heads = []
    for j in range(IDX_HEADS // 2):
        x = _rope(qi_ref[:, j * LANES:(j + 1) * LANES].astype(F32), t64q_ref, 8)
        heads.append(jnp.where(lane < 64, x, 0.0).astype(BF16))
        heads.append(jnp.where(lane < 64, 0.0, x).astype(BF16))
    hg = 4
    qi_groups = [jnp.concatenate(heads[g:g + hg], axis=0) for g in range(0, IDX_HEADS, hg)]
    wt = wt_ref[...] * ((IDX_HEADS ** -0.5) * (IDX_DIM ** -0.5))
    w_cols = [wt[:, IDX_DIM + h:IDX_DIM + h + 1] for h in range(IDX_HEADS)]

    row = qt * tq + lax.broadcasted_iota(I32, (tq, tk), 0)
    col0 = lax.broadcasted_iota(I32, (tq, tk), 1)
    nch = (qt + 1) * (tq // tk)

    def index_keys(c, _):
        off = pl.multiple_of(c * tk, tk)
        kchunk = kis_ref[pl.ds(off, tk), :]
        sc = jnp.zeros((tq, tk), F32)
        for g, qg in enumerate(qi_groups):
            lg = lax.dot_general(qg, kchunk, _NT, preferred_element_type=F32)
            for i in range(hg):
                sc = sc + jnp.maximum(lg[i * tq:(i + 1) * tq], 0.0) * w_cols[g * hg + i]
        keys_ref[c] = jnp.where(col0 + c * tk <= row, sc, -jnp.inf)
        return 0

    lax.fori_loop(0, nch, index_keys, 0)

    def as_float(k):
        return pltpu.bitcast(jnp.where(k < 0, k ^ jnp.int32(0x7FFFFFFF), k), F32)

    def count_ge(cand):
        cf = as_float(cand)

        def body(c, acc):
            a = jnp.where(keys_ref[c] >= cf, 1.0, 0.0)
            return acc + (a[:, :LANES] + a[:, LANES:])
        tot = lax.fori_loop(0, nch, body, jnp.zeros((tq, LANES), F32))
        return jnp.sum(tot, axis=-1, keepdims=True)

    lo = jnp.where(count_ge(jnp.zeros((tq, 1), I32)) >= n_sel, jnp.int32(0), jnp.int32(INT_MIN))

    def bit_step(i, lo):
        cand = lo + lax.shift_left(jnp.int32(1), 30 - i)
        return jnp.where(count_ge(cand) >= n_sel, cand, lo)

    lo = lax.fori_loop(0, 31, bit_step, lo)
    thr = as_float(jnp.maximum(lo, jnp.int32(INT_MIN + 0x00800000)))

    qs = []
    for q_ref in (q0_ref, q1_ref, q2_ref, q3_ref, q4_ref):
        q = _rope(_rms_full(q_ref[...].astype(F32), cqg_ref[...]), t128q_ref, 16)
        qs.append((q * (HEAD_DIM ** -0.5 * LOG2E)).astype(BF16))
    q_all = jnp.concatenate(qs, axis=0)
    rows = C_HEADS * tq

    def select(c, s):
        return jnp.where((keys_ref[c] >= thr)[None], s.reshape(C_HEADS, tq, tk), NEG).reshape(rows, tk)

    o, = _attend([q_all], [kcs_ref], [va_ref], [s_ref], nch - 1, tk, mask_fn=select)
    for h in range(C_HEADS):
        o_ref[:, h * HEAD_DIM:(h + 1) * HEAD_DIM] = o[h * tq:(h + 1) * tq].astype(o_ref.dtype)


def _attn_c(proj, tail, tab128, tab64, cqg, ckg, ikg, *, b, s):
    tq, tk = 256, 256
    nq = s // tq
    n_sel = min(DSA_TOPK, s // 4)
    assert tq % tk == 0 and s % tq == 0
    kern = functools.partial(_attn_c_kernel, tq=tq, tk=tk, nq=nq, n_sel=n_sel)
    small = lambda shape: pl.BlockSpec(shape, lambda bi, qt: (0,) * len(shape))
    qspec = lambda h: pl.BlockSpec((tq, LANES), lambda bi, qt: (bi * nq + qt, QC_C + h))
    return pl.pallas_call(
        kern,
        grid=(b, nq),
        in_specs=[
            qspec(0), qspec(1), qspec(2), qspec(3), qspec(4),
            pl.BlockSpec((tq, IDX_HEADS * IDX_DIM), lambda bi, qt: (bi * nq + qt, QI_C // 8)),
            pl.BlockSpec((s, LANES), lambda bi, qt: (bi, KC_C)),
            pl.BlockSpec((s, LANES), lambda bi, qt: (bi, VC_C)),
            pl.BlockSpec((s, LANES), lambda bi, qt: (bi, 0)),
            pl.BlockSpec((tq, LANES), lambda bi, qt: (bi * nq + qt, 0)),
            pl.BlockSpec((3, tq, LANES), lambda bi, qt: (0, qt, 0)),
            pl.BlockSpec((3, s, LANES), lambda bi, qt: (0, 0, 0)),
            pl.BlockSpec((3, tq, LANES), lambda bi, qt: (0, qt, 0)),
            pl.BlockSpec((3, s, LANES), lambda bi, qt: (0, 0, 0)),
            small((1, LANES)), small((1, LANES)), small((1, LANES)),
        ],
        out_specs=pl.BlockSpec((tq, C_HEADS * HEAD_DIM), lambda bi, qt: (bi * nq + qt, 0)),
        out_shape=jax.ShapeDtypeStruct((b * s, C_HEADS * HEAD_DIM), BF16),
        scratch_shapes=[pltpu.VMEM((s, LANES), BF16), pltpu.VMEM((s, LANES), BF16),
                        pltpu.VMEM((s, 2 * LANES), BF16), pltpu.VMEM((s // tk, tq, tk), F32),
                        pltpu.VMEM((s // tk, C_HEADS * tq, tk), F32)],
        compiler_params=_cp(("parallel", "arbitrary")),
        name="attn_dsa",
    )(proj, proj, proj, proj, proj, proj, proj, proj, tail, tail,
      tab128, tab128, tab64, tab64, cqg, ckg, ikg)


def _out_router_kernel(x_ref, oa_ref, ob_ref, oc_ref, w_ref, g2_ref, wr_ref, br_ref,
                       x1_ref, t_ref, eid_ref, gate_ref):
    na = A_HEADS * HEAD_DIM
    nb = na + B_HEADS * HEAD_DIM
    x1 = (x_ref[...]
          + jnp.dot(oa_ref[...], w_ref[0:na, :], preferred_element_type=F32)
          + jnp.dot(ob_ref[...], w_ref[na:nb, :], preferred_element_type=F32)
          + jnp.dot(oc_ref[...], w_ref[nb:, :], preferred_element_type=F32))
    x1_ref[...] = x1
    t = _rms_full(x1, g2_ref[...])
    t_ref[...] = t

    lg = jnp.dot(t, wr_ref[...], preferred_element_type=F32, precision=lax.Precision.HIGHEST) + br_ref[...]
    lane = lax.broadcasted_iota(I32, lg.shape, 1)
    ninf = -jnp.inf
    gl = jnp.where(lane < N_GROUPS, lg, ninf)
    gm = jnp.max(gl, axis=-1, keepdims=True)
    ge = jnp.exp(gl - gm)
    g_prob = ge / jnp.sum(ge, axis=-1, keepdims=True)
    g_idx = jnp.min(jnp.where(gl == gm, lane, LANES), axis=-1, keepdims=True)
    g_w = jnp.sum(jnp.where(lane == g_idx, g_prob, 0.0), axis=-1, keepdims=True)

    e0 = N_GROUPS + g_idx * EXPERTS_PER_GROUP
    emask = jnp.logical_and(lane >= e0, lane < e0 + EXPERTS_PER_GROUP)
    el = jnp.where(emask, lg, ninf)
    em = jnp.max(el, axis=-1, keepdims=True)
    ee = jnp.exp(el - em)
    ep = jnp.where(emask, ee / jnp.sum(ee, axis=-1, keepdims=True), ninf)
    v1 = jnp.max(ep, axis=-1, keepdims=True)
    i1 = jnp.min(jnp.where(ep == v1, lane, LANES), axis=-1, keepdims=True)
    ep2 = jnp.where(lane == i1, ninf, ep)
    v2 = jnp.max(ep2, axis=-1, keepdims=True)
    i2 = jnp.min(jnp.where(ep2 == v2, lane, LANES), axis=-1, keepdims=True)
    den = v1 + v2
    eid_ref[...] = jnp.where(lane == 0, i1 - N_GROUPS, jnp.where(lane == 1, i2 - N_GROUPS, 0))
    gate_ref[...] = jnp.where(lane == 0, g_w * (v1 / den), jnp.where(lane == 1, g_w * (v2 / den), 0.0))


def _out_router(x, oa, ob, oc, w_out_bf, g2, w_router, b_router):
    t, d = x.shape
    tm = 256
    row = lambda c: pl.BlockSpec((tm, c), lambda i: (i, 0))
    full = lambda r, c: pl.BlockSpec((r, c), lambda i: (0, 0))
    return pl.pallas_call(
        _out_router_kernel,
        grid=(t // tm,),
        in_specs=[row(d), row(oa.shape[1]), row(ob.shape[1]), row(oc.shape[1]),
                  full(d, d), full(1, d), full(d, LANES), full(1, LANES)],
        out_specs=[row(d), row(d), row(LANES), row(LANES)],
        out_shape=[jax.ShapeDtypeStruct((t, d), F32), jax.ShapeDtypeStruct((t, d), F32),
                   jax.ShapeDtypeStruct((t, LANES), I32), jax.ShapeDtypeStruct((t, LANES), F32)],
        compiler_params=_cp(("parallel",)),
        name="out_proj_router",
    )(x, oa, ob, oc, w_out_bf, g2.reshape(1, d), w_router, b_router)


def _moe_kernel(texp_ref, nused_ref, rtok_ref, t_hbm, wg_ref, wu_ref, wd_ref,
                y_ref, xbuf, sem, wg_bf, wu_bf, wd_bf, *, tm):
    i = pl.program_id(0)
    nu = nused_ref[0]

    def gather(tile, slot):
        def body(r, _):
            tok = rtok_ref[tile * tm + r]
            pltpu.make_async_copy(t_hbm.at[pl.ds(tok, 1)], xbuf.at[slot, pl.ds(r, 1)], sem.at[slot]).start()
            return 0
        lax.fori_loop(0, tm, body, 0, unroll=8)

    @pl.when(i == 0)
    def _():
        gather(0, 0)

    @pl.when(i + 1 < nu)
    def _():
        gather(i + 1, (i + 1) % 2)

    @pl.when(i < nu)
    def _():
        slot = i % 2
        pltpu.make_async_copy(xbuf.at[slot], xbuf.at[slot], sem.at[slot]).wait()

        changed = jnp.logical_or(i == 0, texp_ref[i] != texp_ref[jnp.maximum(i - 1, 0)])

        @pl.when(changed)
        def _():
            wg_bf[...] = wg_ref[...].astype(BF16)
            wu_bf[...] = wu_ref[...].astype(BF16)
            wd_bf[...] = wd_ref[...].astype(BF16)

        x = xbuf[slot].astype(BF16)
        g = jnp.dot(x, wg_bf[...], preferred_element_type=F32)
        u = jnp.dot(x, wu_bf[...], preferred_element_type=F32)
        h = (g * jax.nn.sigmoid(g)) * u
        y_ref[...] = jnp.dot(h.astype(BF16), wd_bf[...], preferred_element_type=F32)

    @pl.when(i >= nu)
    def _():
        y_ref[...] = jnp.zeros_like(y_ref)


def _moe(t, tile_expert, n_used, row_token, w_gate, w_up, w_down, *, layer, n_tiles):
    tm = MOE_TILE
    d = t.shape[1]
    grid_spec = pltpu.PrefetchScalarGridSpec(
        num_scalar_prefetch=3,
        grid=(n_tiles,),
        in_specs=[
            pl.BlockSpec(memory_space=pl.ANY),
            pl.BlockSpec((None, None, d, EXPERT_FF), lambda i, te, nu, rt: (layer, te[i], 0, 0)),
            pl.BlockSpec((None, None, d, EXPERT_FF), lambda i, te, nu, rt: (layer, te[i], 0, 0)),
            pl.BlockSpec((None, None, EXPERT_FF, d), lambda i, te, nu, rt: (layer, te[i], 0, 0)),
        ],
        out_specs=pl.BlockSpec((tm, d), lambda i, te, nu, rt: (i, 0)),
        scratch_shapes=[pltpu.VMEM((2, tm, d), F32), pltpu.SemaphoreType.DMA((2,)),
                        pltpu.VMEM((d, EXPERT_FF), BF16), pltpu.VMEM((d, EXPERT_FF), BF16),
                        pltpu.VMEM((EXPERT_FF, d), BF16)],
    )
    return pl.pallas_call(
        functools.partial(_moe_kernel, tm=tm),
        grid_spec=grid_spec,
        out_shape=jax.ShapeDtypeStruct((n_tiles * tm, d), F32),
        compiler_params=_cp(("arbitrary",)),
        name="moe_experts",
    )(tile_expert, n_used, row_token, t, w_gate, w_up, w_down)


def _combine_kernel(pos_ref, x1_ref, gate_ref, gn_ref, y_hbm, o_ref, *rest, tc):
    h_ref = rest[0] if len(rest) == 3 else None
    buf, sem = rest[-2:]
    i = pl.program_id(0)
    n = pl.num_programs(0)

    def gather(tile, slot):
        def body(r, _):
            a = (tile * tc + r) * 2
            pltpu.make_async_copy(y_hbm.at[pl.ds(pos_ref[a], 1)], buf.at[slot, 0, pl.ds(r, 1)],
                                  sem.at[slot]).start(priority=0)
            pltpu.make_async_copy(y_hbm.at[pl.ds(pos_ref[a + 1], 1)], buf.at[slot, 1, pl.ds(r, 1)],
                                  sem.at[slot]).start(priority=1)
            return 0
        lax.fori_loop(0, tc, body, 0, unroll=8)

    @pl.when(i == 0)
    def _():
        gather(0, 0)

    @pl.when(i + 1 < n)
    def _():
        gather(i + 1, (i + 1) % 2)

    slot = i % 2
    pltpu.make_async_copy(buf.at[slot], buf.at[slot], sem.at[slot]).wait()
    gt = gate_ref[...]
    x2 = x1_ref[...] + gt[:, 0:1] * buf[slot, 0] + gt[:, 1:2] * buf[slot, 1]
    o_ref[...] = x2
    if h_ref is not None:
        h_ref[...] = _rms_full(x2, gn_ref[...]).astype(h_ref.dtype)


def _combine(pos, x1, gate, y, g_next, emit_h):
    t, d = x1.shape
    tc = 256
    row = pl.BlockSpec((tc, d), lambda i, p: (i, 0))
    grid_spec = pltpu.PrefetchScalarGridSpec(
        num_scalar_prefetch=1,
        grid=(t // tc,),
        in_specs=[row, pl.BlockSpec((tc, LANES), lambda i, p: (i, 0)), pl.BlockSpec((1, d), lambda i, p: (0, 0)),
                  pl.BlockSpec(memory_space=pl.ANY)],
        out_specs=[row, row] if emit_h else row,
        scratch_shapes=[pltpu.VMEM((2, 2, tc, d), F32), pltpu.SemaphoreType.DMA((2,))],
    )
    x2_shape = jax.ShapeDtypeStruct((t, d), F32)
    return pl.pallas_call(
        functools.partial(_combine_kernel, tc=tc),
        grid_spec=grid_spec,
        out_shape=[x2_shape, jax.ShapeDtypeStruct((t, d), BF16)] if emit_h else x2_shape,
        compiler_params=_cp(("arbitrary",)),
        name="moe_combine",
    )(pos, x1, gate, g_next.reshape(1, d), y)


def _rope_lane_tables(seq, dim):
    rot = dim // ROPE_FRAC
    half = rot // 2
    inv = 1.0 / (ROPE_THETA ** (jnp.arange(0, rot, 2, dtype=F32) / rot))
    ang = jnp.arange(seq, dtype=F32)[:, None] * inv[None, :]
    cos, sin = jnp.cos(ang), jnp.sin(ang)
    z_half = jnp.zeros((seq, half), F32)
    z_rest = jnp.zeros((seq, dim - rot), F32)
    c = jnp.concatenate([cos, cos, jnp.ones((seq, dim - rot), F32)], axis=-1)
    sa = jnp.concatenate([-sin, z_half, z_rest], axis=-1)
    sb = jnp.concatenate([z_half, sin, z_rest], axis=-1)
    reps = LANES // dim
    return jnp.stack([jnp.tile(c, (1, reps)), jnp.tile(sa, (1, reps)), jnp.tile(sb, (1, reps))])


def _routing_plan(eid, n_tiles):
    tm = MOE_TILE
    e = eid[:, :2].reshape(-1)
    n_assign = e.shape[0]
    onehot = (e[:, None] == jnp.arange(N_EXPERTS, dtype=I32)[None, :]).astype(I32)
    csum = jnp.cumsum(onehot, axis=0)
    rank = jnp.take_along_axis(csum, e[:, None], axis=1)[:, 0] - 1
    counts = csum[-1]
    tiles_per = (counts + tm - 1) // tm
    tile_end = jnp.cumsum(tiles_per)
    tile_start = tile_end - tiles_per
    n_used = tile_end[-1]
    pos = tile_start[e] * tm + rank
    tile_ids = jnp.minimum(jnp.arange(n_tiles, dtype=I32), n_used - 1)
    tile_expert = jnp.sum((tile_end[None, :] <= tile_ids[:, None]).astype(I32), axis=1)
    row_token = jnp.zeros((n_tiles * tm,), I32).at[pos].set(jnp.arange(n_assign, dtype=I32) // 2)
    return tile_expert, n_used.reshape(1).astype(I32), row_token, pos.astype(I32)


def kernel(x, norm1_g, w_in, a_qn_g, a_kn_g, a_lambda, a_subln_g, b_qn_g, b_kn_g, c_qn_g, c_kn_g,
           idx_kn_g, w_out, norm2_g, w_group, b_group, w_expert, b_expert, w_gate, w_up, w_down):
    b, s, d = x.shape
    depth = w_in.shape[0]
    assert d == D_MODEL and s % MOBA_BLOCK == 0
    t = b * s
    n_tiles = (2 * t) // MOE_TILE + N_EXPERTS
    tab64 = _rope_lane_tables(s, A_QK)
    tab128 = _rope_lane_tables(s, HEAD_DIM)
    tile2 = lambda v: jnp.tile(v, 2).reshape(1, LANES)
    row = lambda v: v.reshape(1, LANES)

    w_main = w_in[:, :, :MAIN_COLS].astype(BF16)
    w_tail = jnp.pad(w_in[:, :, MAIN_COLS:], ((0, 0), (0, 0), (0, LANES - TAIL_COLS)))

    xf = x.reshape(t, d)
    h = _rmsnorm_bf16(xf, norm1_g[0])
    for l in range(depth):
        lam_init = 0.8 - 0.6 * math.exp(-0.3 * l)
        proj = _matmul(h, w_main, l, BF16, 1024, 1024, "in_proj")
        tail = _matmul(h, w_tail, l, F32, 1024, LANES, "in_proj_tail")

        oa = _attn_a(proj, tab64, tile2(a_qn_g[l]), tile2(a_kn_g[l]), a_lambda[l], row(a_subln_g[l]),
                     b=b, s=s, lam_init=lam_init)
        ob = _attn_b(proj, tab128, row(b_qn_g[l]), row(b_kn_g[l]), b=b, s=s)
        ikg = jnp.pad(idx_kn_g[l], (0, LANES - IDX_DIM)).reshape(1, LANES)
        oc = _attn_c(proj, tail, tab128, tab64, row(c_qn_g[l]), row(c_kn_g[l]), ikg, b=b, s=s)

        w_router = jnp.pad(jnp.concatenate([w_group[l], w_expert[l]], axis=1),
                           ((0, 0), (0, LANES - N_GROUPS - N_EXPERTS)))
        b_router = jnp.pad(jnp.concatenate([b_group[l], b_expert[l]]),
                           (0, LANES - N_GROUPS - N_EXPERTS)).reshape(1, LANES)
        x1, tn, eid, gate = _out_router(xf, oa, ob, oc, _cast_bf16(w_out, l, d), norm2_g[l], w_router, b_router)

        tile_expert, n_used, row_token, pos = _routing_plan(eid, n_tiles)
        y = _moe(tn, tile_expert, n_used, row_token, w_gate, w_up, w_down, layer=l, n_tiles=n_tiles)
        if l + 1 < depth:
            xf, h = _combine(pos, x1, gate, y, norm1_g[l + 1], True)
        else:
            xf = _combine(pos, x1, gate, y, norm1_g[l], False)
    return xf.reshape(b, s, d)
```
